```python
import math
import jax, jax.numpy as jnp
from jax import lax
import numpy as np

D_MODEL = 1024
BATCH = 8
SEQ = 2048
DEPTH = 4
DEC_BATCH = 128
DEC_SEQ = 8
PAST_LEN = 16384
PAGE_SIZE = 128

N_MIXERS = 2
N_A = (DEPTH + 1) // 2
N_B = DEPTH // 2
H_DN = 8
DK = 128
DV = 128
QKV_DIM = 2 * H_DN * DK + H_DN * DV
A_IN_DIM = QKV_DIM + H_DN * DV + 2 * H_DN
DN_CONV = 4
CHUNK = 64
SC_CONV = 3
D_FF = 2816
FFN_CONV = 3
D_PLE = 256
ALPHA = (2.0 * DEPTH) ** 0.25
BETA_INIT = (8.0 * DEPTH) ** -0.25
LN_EPS = 1e-5
RMS_EPS = 1e-6
L2_EPS = 1e-6

kernel_name = 'hybrid_deltanet_shortconv_convffn_step'


def layer_norm(x, g, b):
    xf = x.astype(jnp.float32)
    mu = jnp.mean(xf, -1, keepdims=True)
    var = jnp.mean(jnp.square(xf - mu), -1, keepdims=True)
    y = (xf - mu) * lax.rsqrt(var + LN_EPS) * g.astype(jnp.float32) + b.astype(jnp.float32)
    return y.astype(x.dtype)


def causal_dwconv(u, w, buf):
    W = w.shape[0]
    L = u.shape[1]
    ext = jnp.concatenate([buf.astype(u.dtype), u], axis=1)
    y = ext[:, 0:L] * w[0]
    for j in range(1, W):
        y = y + ext[:, j:j + L] * w[j]
    return y, ext[:, ext.shape[1] - (W - 1):]


def l2norm(t):
    return t * lax.rsqrt(jnp.sum(t * t, -1, keepdims=True) + L2_EPS)


def gated_delta_chunked(q, k, v, g, beta, S0):
    B, L, H, _ = q.shape
    C = min(CHUNK, L)
    pad = (-L) % C
    if pad:
        pw = ((0, 0), (0, pad), (0, 0), (0, 0))
        q = jnp.pad(q, pw); k = jnp.pad(k, pw); v = jnp.pad(v, pw)
        g = jnp.pad(g, pw[:3]); beta = jnp.pad(beta, pw[:3])
    N = (L + pad) // C

    def chunks(t):
        t = t.reshape((B, N, C, H) + t.shape[3:])
        return jnp.moveaxis(t, (1, 3), (0, 2))

    q = chunks(q) * (DK ** -0.5)
    k = chunks(k)
    v = chunks(v)
    gc = jnp.cumsum(chunks(g), axis=-1)
    bt = chunks(beta)[..., None]
    kb = k * bt
    vb = v * bt
    idx = jnp.arange(C)
    causal = idx[:, None] >= idx[None, :]
    strict = idx[:, None] > idx[None, :]
    decay = jnp.exp(jnp.where(causal, gc[..., :, None] - gc[..., None, :], -jnp.inf))
    A = jnp.where(strict, jnp.einsum('nbhid,nbhjd->nbhij', kb, k) * decay, 0.0)
    rhs = jnp.concatenate([vb, kb * jnp.exp(gc)[..., None]], axis=-1)
    sol = lax.linalg.triangular_solve(A + jnp.eye(C, dtype=A.dtype), rhs,
                                      left_side=True, lower=True, unit_diagonal=True)
    u = sol[..., :DV]
    w = sol[..., DV:]
    att = jnp.einsum('nbhid,nbhjd->nbhij', q, k) * decay

    def step(S, xs):
        q_c, k_c, u_c, w_c, gc_c, att_c = xs
        v_new = u_c - jnp.einsum('bhck,bhkv->bhcv', w_c, S)
        o = (jnp.einsum('bhck,bhkv->bhcv', q_c * jnp.exp(gc_c)[..., None], S)
             + jnp.einsum('bhij,bhjv->bhiv', att_c, v_new))
        g_last = gc_c[..., -1]
        S = (S * jnp.exp(g_last)[..., None, None]
             + jnp.einsum('bhck,bhcv->bhkv', k_c * jnp.exp(g_last[..., None] - gc_c)[..., None], v_new))
        return S, o

    S, o = lax.scan(step, S0, (q, k, u, w, gc, att))
    o = jnp.moveaxis(o, (0, 2), (1, 3)).reshape(B, N * C, H, DV)[:, :L]
    return o, S


def delta_mixer(x, conv_buf, S0, w_in, conv_w, a_log, dt_bias, norm_g, w_out):
    B, L, _ = x.shape
    proj = x @ w_in
    qkv = proj[..., :QKV_DIM]
    z = proj[..., QKV_DIM:QKV_DIM + H_DN * DV]
    ba = proj[..., QKV_DIM + H_DN * DV:].astype(jnp.float32)
    qkv_c, new_buf = causal_dwconv(qkv, conv_w, conv_buf)
    qkv_c = jax.nn.silu(qkv_c.astype(jnp.float32))
    q = l2norm(qkv_c[..., :H_DN * DK].reshape(B, L, H_DN, DK))
    k = l2norm(qkv_c[..., H_DN * DK:2 * H_DN * DK].reshape(B, L, H_DN, DK))
    v = qkv_c[..., 2 * H_DN * DK:].reshape(B, L, H_DN, DV)
    beta = jax.nn.sigmoid(ba[..., :H_DN])
    g = -jnp.exp(a_log.astype(jnp.float32)) * jax.nn.softplus(ba[..., H_DN:] + dt_bias.astype(jnp.float32))
    o, S = gated_delta_chunked(q, k, v, g, beta, S0.astype(jnp.float32))
    o = (o * lax.rsqrt(jnp.mean(o * o, -1, keepdims=True) + RMS_EPS) * norm_g.astype(jnp.float32)
         * jax.nn.silu(z.astype(jnp.float32).reshape(B, L, H_DN, DV)))
    y = o.reshape(B, L, H_DN * DV).astype(x.dtype) @ w_out
    return y, new_buf, S.astype(S0.dtype)


def shortconv_mixer(x, buf, w_in, conv_w, w_out):
    proj = x @ w_in
    b = proj[..., :D_MODEL]
    c = proj[..., D_MODEL:2 * D_MODEL]
    h = proj[..., 2 * D_MODEL:]
    uc, new_buf = causal_dwconv(c * h, conv_w, buf)
    return (b * uc) @ w_out, new_buf


def conv_ffn(x, buf, w_up, conv_w, w_down):
    hc, new_buf = causal_dwconv(x @ w_up, conv_w, buf)
    return (jax.nn.silu(hc[..., D_FF:]) * hc[..., :D_FF]) @ w_down, new_buf


def trunk(x, p, dn_conv, dn_S, sc_conv, ffn_conv,
          dn_w_in, dn_conv_w, dn_a_log, dn_dt_bias, dn_norm_g, dn_w_out,
          sc_w_in, sc_conv_w, sc_w_out, ffn_w_up, ffn_conv_w, ffn_w_down,
          ln_mix_g, ln_mix_b, ln_ffn_g, ln_ffn_b, ple_w_proj, ple_w_gate):
    new_dn_conv, new_dn_S, new_sc, new_ffn = [], [], [], []
    for i in range(DEPTH):
        j = i // N_MIXERS
        if i % N_MIXERS == 0:
            mix, cb, S = delta_mixer(x, dn_conv[j], dn_S[j], dn_w_in[j], dn_conv_w[j], dn_a_log[j],
                                     dn_dt_bias[j], dn_norm_g[j], dn_w_out[j])
            new_dn_conv.append(cb)
            new_dn_S.append(S)
        else:
            mix, sb = shortconv_mixer(x, sc_conv[j], sc_w_in[j], sc_conv_w[j], sc_w_out[j])
            new_sc.append(sb)
        x = layer_norm(ALPHA * x + mix, ln_mix_g[i], ln_mix_b[i])
        f, fb = conv_ffn(x, ffn_conv[i], ffn_w_up[i], ffn_conv_w[i], ffn_w_down[i])
        new_ffn.append(fb)
        x = layer_norm(ALPHA * x + f, ln_ffn_g[i], ln_ffn_b[i])
        x = x + (p[i].astype(x.dtype) @ ple_w_proj[i]) * jax.nn.sigmoid(x @ ple_w_gate[i])
    return x, jnp.stack(new_dn_conv), jnp.stack(new_dn_S), jnp.stack(new_sc), jnp.stack(new_ffn)


def setup_inputs(seed: int = 0) -> dict:
    key = jax.random.key(seed)
    ks = iter(jax.random.split(key, 40))
    f32 = jnp.float32

    def nrm(shape, scale):
        return jax.random.normal(next(ks), shape, f32) * scale

    x_prompt = nrm((BATCH, SEQ, D_MODEL), 1.0)
    x_sample = nrm((DEC_BATCH, DEC_SEQ, D_MODEL), 1.0)
    p_prompt = nrm((DEPTH, BATCH, SEQ, D_PLE), 1.0)
    p_sample = nrm((DEPTH, DEC_BATCH, DEC_SEQ, D_PLE), 1.0)
    state_dn_conv = nrm((N_A, DEC_BATCH, DN_CONV - 1, QKV_DIM), 1.0)
    state_dn_S = nrm((N_A, DEC_BATCH, H_DN, DK, DV), 0.05)
    state_sc_conv = nrm((N_B, DEC_BATCH, SC_CONV - 1, D_MODEL), 1.0)
    state_ffn_conv = nrm((DEPTH, DEC_BATCH, FFN_CONV - 1, 2 * D_FF), 1.0)

    dn_w_in = nrm((N_A, D_MODEL, A_IN_DIM), D_MODEL ** -0.5)
    dn_conv_w = nrm((N_A, DN_CONV, QKV_DIM), DN_CONV ** -0.5)
    dn_a_log = jnp.log(jax.random.uniform(next(ks), (N_A, H_DN), f32, 1.0, 16.0))
    dt = jnp.exp(jax.random.uniform(next(ks), (N_A, H_DN), f32, math.log(1e-3), math.log(1e-1)))
    dn_dt_bias = dt + jnp.log(-jnp.expm1(-dt))
    dn_norm_g = 1.0 + nrm((N_A, DV), 0.02)
    dn_w_out = nrm((N_A, H_DN * DV, D_MODEL), (H_DN * DV) ** -0.5 * BETA_INIT)

    sc_w_in = nrm((N_B, D_MODEL, 3 * D_MODEL), D_MODEL ** -0.5)
    sc_conv_w = nrm((N_B, SC_CONV, D_MODEL), SC_CONV ** -0.5)
    sc_w_out = nrm((N_B, D_MODEL, D_MODEL), D_MODEL ** -0.5 * BETA_INIT)

    ffn_w_up = nrm((DEPTH, D_MODEL, 2 * D_FF), D_MODEL ** -0.5)
    ffn_conv_w = nrm((DEPTH, FFN_CONV, 2 * D_FF), FFN_CONV ** -0.5)
    ffn_w_down = nrm((DEPTH, D_FF, D_MODEL), D_FF ** -0.5 * BETA_INIT)

    ln_mix_g = 1.0 + nrm((DEPTH, D_MODEL), 0.02)
    ln_mix_b = nrm((DEPTH, D_MODEL), 0.02)
    ln_ffn_g = 1.0 + nrm((DEPTH, D_MODEL), 0.02)
    ln_ffn_b = nrm((DEPTH, D_MODEL), 0.02)
    ple_w_proj = nrm((DEPTH, D_PLE, D_MODEL), D_PLE ** -0.5)
    ple_w_gate = nrm((DEPTH, D_MODEL, D_MODEL), D_MODEL ** -0.5)

    return {'x_prompt': x_prompt, 'x_sample': x_sample, 'p_prompt': p_prompt, 'p_sample': p_sample,
            'state_dn_conv': state_dn_conv, 'state_dn_S': state_dn_S,
            'state_sc_conv': state_sc_conv, 'state_ffn_conv': state_ffn_conv,
            'dn_w_in': dn_w_in, 'dn_conv_w': dn_conv_w, 'dn_a_log': dn_a_log, 'dn_dt_bias': dn_dt_bias,
            'dn_norm_g': dn_norm_g, 'dn_w_out': dn_w_out,
            'sc_w_in': sc_w_in, 'sc_conv_w': sc_conv_w, 'sc_w_out': sc_w_out,
            'ffn_w_up': ffn_w_up, 'ffn_conv_w': ffn_conv_w, 'ffn_w_down': ffn_w_down,
            'ln_mix_g': ln_mix_g, 'ln_mix_b': ln_mix_b, 'ln_ffn_g': ln_ffn_g, 'ln_ffn_b': ln_ffn_b,
            'ple_w_proj': ple_w_proj, 'ple_w_gate': ple_w_gate}


def reference(x_prompt, x_sample, p_prompt, p_sample, state_dn_conv, state_dn_S, state_sc_conv, state_ffn_conv,
              dn_w_in, dn_conv_w, dn_a_log, dn_dt_bias, dn_norm_g, dn_w_out,
              sc_w_in, sc_conv_w, sc_w_out, ffn_w_up, ffn_conv_w, ffn_w_down,
              ln_mix_g, ln_mix_b, ln_ffn_g, ln_ffn_b, ple_w_proj, ple_w_gate):
    Bp = x_prompt.shape[0]
    dt_act = x_prompt.dtype
    z_dn_conv = jnp.zeros((N_A, Bp, DN_CONV - 1, QKV_DIM), dt_act)
    z_dn_S = jnp.zeros((N_A, Bp, H_DN, DK, DV), state_dn_S.dtype)
    z_sc_conv = jnp.zeros((N_B, Bp, SC_CONV - 1, D_MODEL), dt_act)
    z_ffn_conv = jnp.zeros((DEPTH, Bp, FFN_CONV - 1, 2 * D_FF), dt_act)

    y_prompt, prompt_dn_conv, prompt_dn_S, prompt_sc_conv, prompt_ffn_conv = trunk(
        x_prompt, p_prompt, z_dn_conv, z_dn_S, z_sc_conv, z_ffn_conv,
        dn_w_in, dn_conv_w, dn_a_log, dn_dt_bias, dn_norm_g, dn_w_out,
        sc_w_in, sc_conv_w, sc_w_out, ffn_w_up, ffn_conv_w, ffn_w_down,
        ln_mix_g, ln_mix_b, ln_ffn_g, ln_ffn_b, ple_w_proj, ple_w_gate)

    y_sample, sample_dn_conv, sample_dn_S, sample_sc_conv, sample_ffn_conv = trunk(
        x_sample, p_sample, state_dn_conv, state_dn_S, state_sc_conv, state_ffn_conv,
        dn_w_in, dn_conv_w, dn_a_log, dn_dt_bias, dn_norm_g, dn_w_out,
        sc_w_in, sc_conv_w, sc_w_out, ffn_w_up, ffn_conv_w, ffn_w_down,
        ln_mix_g, ln_mix_b, ln_ffn_g, ln_ffn_b, ple_w_proj, ple_w_gate)

    return (y_prompt, y_sample,
            prompt_dn_conv, prompt_dn_S, prompt_sc_conv, prompt_ffn_conv,
            sample_dn_conv, sample_dn_S, sample_sc_conv, sample_ffn_conv)
```

```python
import functools
import math

import jax
import jax.numpy as jnp
from jax import lax
from jax.experimental import pallas as pl
from jax.experimental.pallas import tpu as pltpu

F32 = jnp.float32
BF16 = jnp.bfloat16

D_MODEL = 1024
DEPTH = 4
N_HEADS = 8
D_HEAD = 128
QKV_DIM = 3 * N_HEADS * D_HEAD
DN_CONV = 4
CHUNK = 64
SC_CONV = 3
D_FF = 2816
FFN_CONV = 3
D_PLE = 256
ALPHA = (2.0 * DEPTH) ** 0.25
LN_EPS = 1e-5
RMS_EPS = 1e-6
L2_EPS = 1e-6

LANES = 128
SUBLANES = 8
VMEM_LIMIT_BYTES = 56 * 1024 * 1024
HDR_ROWS = 16
FFN_COLS = 256
MIX_COLS = 512


def _bdot(a, b):
    return jnp.dot(a.astype(BF16), b.astype(BF16), preferred_element_type=F32)


def _bdot_nt(a, b):
    return lax.dot_general(a.astype(BF16), b.astype(BF16), (((1,), (1,)), ((), ())),
                           preferred_element_type=F32)


def _bdot_tn(a, b):
    return lax.dot_general(a.astype(BF16), b.astype(BF16), (((0,), (0,)), ((), ())),
                           preferred_element_type=F32)


def _split3(x):
    hi = x.astype(BF16)
    r = x - hi.astype(F32)
    mid = r.astype(BF16)
    lo = (r - mid.astype(F32)).astype(BF16)
    return hi, mid, lo


def _dot3(x, e):
    hi, mid, lo = _split3(x)
    return (jnp.dot(hi, e, preferred_element_type=F32) + jnp.dot(mid, e, preferred_element_type=F32)
            + jnp.dot(lo, e, preferred_element_type=F32))


def _dot3_l(e, x):
    hi, mid, lo = _split3(x)
    return (jnp.dot(e, hi, preferred_element_type=F32) + jnp.dot(e, mid, preferred_element_type=F32)
            + jnp.dot(e, lo, preferred_element_type=F32))


def _layer_norm(x, g, b):
    mu = jnp.mean(x, axis=-1, keepdims=True)
    xc = x - mu
    var = jnp.mean(xc * xc, axis=-1, keepdims=True)
    return xc * lax.rsqrt(var + LN_EPS) * g + b


def _silu(x):
    return x * jax.nn.sigmoid(x)


def _softplus(x):
    return jnp.maximum(x, 0.0) + jnp.log1p(jnp.exp(-jnp.abs(x)))


def _conv_carry(u, cols, width, cw_ref, ext_ref, carry_ref, tail_ref):
    tm = u.shape[0]
    ext_ref[0:HDR_ROWS, :] = carry_ref[:, cols]
    ext_ref[HDR_ROWS:HDR_ROWS + tm, :] = u
    y = None
    for j in range(width):
        d = width - 1 - j
        src = u if d == 0 else ext_ref[HDR_ROWS - d:HDR_ROWS - d + tm, :]
        term = src * cw_ref[j:j + 1, cols]
        y = term if y is None else y + term
    carry_ref[:, cols] = u[tm - HDR_ROWS:tm, :]
    tail_ref[:, cols] = u[tm - SUBLANES:tm, :]
    return y


def _conv_grouped(u, hdr, cols, width, cw_ref, tail_ref, period):
    n = u.shape[0]
    assert period & (period - 1) == 0
    t = lax.broadcasted_iota(jnp.int32, u.shape, 0) & (period - 1)
    y = None
    for j in range(width):
        d = width - 1 - j
        if d == 0:
            src = u
        else:
            src = jnp.where(t >= d, pltpu.roll(u, d, 0), pltpu.roll(hdr, (n - period + d) % n, 0))
        term = src * cw_ref[j:j + 1, cols]
        y = term if y is None else y + term
    tail_ref[:, cols] = u
    return y


def _residual_ln_ple(x, f, p, g_ref, b_ref, wproj_ref, wgate_ref):
    xn = _layer_norm(ALPHA * x + f, g_ref[...], b_ref[...])
    gate = jax.nn.sigmoid(_bdot(xn, wgate_ref[...]))
    return xn + _bdot(p, wproj_ref[...]) * gate


def _ffn_kernel(*refs, tm, grouped, tiles_per_seq):
    if grouped:
        (x_ref, hdr_ref, p_ref, wup_ref, cw_ref, wdn_ref, g_ref, b_ref, wproj_ref, wgate_ref,
         y_ref, tail_ref, h_s) = refs
    else:
        (x_ref, p_ref, wup_ref, cw_ref, wdn_ref, g_ref, b_ref, wproj_ref, wgate_ref,
         y_ref, tail_ref, h_s, ext_s, carry_s) = refs

        @pl.when(pl.program_id(0) % tiles_per_seq == 0)
        def _():
            carry_s[...] = jnp.zeros_like(carry_s)

    x = x_ref[...]
    xb = x.astype(BF16)

    def conv(u, c0):
        cols = slice(c0, c0 + FFN_COLS)
        if grouped:
            return _conv_grouped(u, hdr_ref[:, cols], cols, FFN_CONV, cw_ref, tail_ref, SUBLANES)
        return _conv_carry(u, cols, FFN_CONV, cw_ref, ext_s, carry_s, tail_ref)

    for c in range(D_FF // FFN_COLS):
        c0 = c * FFN_COLS
        val = conv(jnp.dot(xb, wup_ref[:, c0:c0 + FFN_COLS], preferred_element_type=F32), c0)
        gat = conv(jnp.dot(xb, wup_ref[:, D_FF + c0:D_FF + c0 + FFN_COLS], preferred_element_type=F32),
                   D_FF + c0)
        h_s[:, c0:c0 + FFN_COLS] = (_silu(gat) * val).astype(BF16)

    f = jnp.dot(h_s[...], wdn_ref[...], preferred_element_type=F32)
    y_ref[...] = _residual_ln_ple(x, f, p_ref[...], g_ref, b_ref, wproj_ref, wgate_ref)


def _const_spec(shape):
    zeros = (0,) * len(shape)
    return pl.BlockSpec(shape, lambda i: zeros, pipeline_mode=pl.Buffered(1))


def _ffn_call(x, hdr, p, wup, cw, wdn, g, b, wproj, wgate, *, tm, seq_len):
    rows = x.shape[0]
    grouped = hdr is not None
    row_spec = lambda w: pl.BlockSpec((tm, w), lambda i: (i, 0))
    weights = [wup, cw, wdn, g, b, wproj, wgate]
    in_specs = [row_spec(D_MODEL)] + ([row_spec(2 * D_FF)] if grouped else []) + [row_spec(D_PLE)]
    in_specs += [_const_spec(w.shape) for w in weights]
    args = [x] + ([hdr] if grouped else []) + [p] + weights
    scratch = [pltpu.VMEM((tm, D_FF), BF16)]
    if grouped:
        tail_rows = rows
        tail_spec = row_spec(2 * D_FF)
        tiles_per_seq = 1
    else:
        tiles_per_seq = seq_len // tm
        tail_rows = rows // seq_len * SUBLANES
        tail_spec = pl.BlockSpec((SUBLANES, 2 * D_FF), lambda i: (i // tiles_per_seq, 0))
        scratch += [pltpu.VMEM((HDR_ROWS + tm, FFN_COLS), F32), pltpu.VMEM((HDR_ROWS, 2 * D_FF), F32)]
    return pl.pallas_call(
        functools.partial(_ffn_kernel, tm=tm, grouped=grouped, tiles_per_seq=tiles_per_seq),
        grid=(rows // tm,),
        in_specs=in_specs,
        out_specs=[row_spec(D_MODEL), tail_spec],
        out_shape=[jax.ShapeDtypeStruct((rows, D_MODEL), F32),
                   jax.ShapeDtypeStruct((tail_rows, 2 * D_FF), F32)],
        scratch_shapes=scratch,
        compiler_params=pltpu.CompilerParams(dimension_semantics=("arbitrary",),
                                             vmem_limit_bytes=VMEM_LIMIT_BYTES),
        name="ffn_grouped" if grouped else "ffn_seq",
    )(*args)


def _sc_kernel(*refs, tm, grouped, tiles_per_seq):
    if grouped:
        x_ref, hdr_ref, win_ref, cw_ref, wout_ref, g_ref, b_ref, y_ref, tail_ref, h_s = refs
    else:
        x_ref, win_ref, cw_ref, wout_ref, g_ref, b_ref, y_ref, tail_ref, h_s, ext_s, carry_s = refs

        @pl.when(pl.program_id(0) % tiles_per_seq == 0)
        def _():
            carry_s[...] = jnp.zeros_like(carry_s)

    x = x_ref[...]
    xb = x.astype(BF16)
    for c in range(D_MODEL // MIX_COLS):
        c0 = c * MIX_COLS
        cols = slice(c0, c0 + MIX_COLS)
        bb = jnp.dot(xb, win_ref[:, c0:c0 + MIX_COLS], preferred_element_type=F32)
        cc = jnp.dot(xb, win_ref[:, D_MODEL + c0:D_MODEL + c0 + MIX_COLS], preferred_element_type=F32)
        hh = jnp.dot(xb, win_ref[:, 2 * D_MODEL + c0:2 * D_MODEL + c0 + MIX_COLS], preferred_element_type=F32)
        u = cc * hh
        if grouped:
            uc = _conv_grouped(u, hdr_ref[:, cols], cols, SC_CONV, cw_ref, tail_ref, SUBLANES)
        else:
            uc = _conv_carry(u, cols, SC_CONV, cw_ref, ext_s, carry_s, tail_ref)
        h_s[:, cols] = (bb * uc).astype(BF16)
    mix = jnp.dot(h_s[...], wout_ref[...], preferred_element_type=F32)
    y_ref[...] = _layer_norm(ALPHA * x + mix, g_ref[...], b_ref[...])


def _sc_call(x, hdr, win, cw, wout, g, b, *, tm, seq_len):
    rows = x.shape[0]
    grouped = hdr is not None
    row_spec = lambda w: pl.BlockSpec((tm, w), lambda i: (i, 0))
    weights = [win, cw, wout, g, b]
    in_specs = [row_spec(D_MODEL)] + ([row_spec(D_MODEL)] if grouped else [])
    in_specs += [_const_spec(w.shape) for w in weights]
    args = [x] + ([hdr] if grouped else []) + weights
    scratch = [pltpu.VMEM((tm, D_MODEL), BF16)]
    if grouped:
        tail_rows = rows
        tail_spec = row_spec(D_MODEL)
        tiles_per_seq = 1
    else:
        tiles_per_seq = seq_len // tm
        tail_rows = rows // seq_len * SUBLANES
        tail_spec = pl.BlockSpec((SUBLANES, D_MODEL), lambda i: (i // tiles_per_seq, 0))
        scratch += [pltpu.VMEM((HDR_ROWS + tm, MIX_COLS), F32), pltpu.VMEM((HDR_ROWS, D_MODEL), F32)]
    return pl.pallas_call(
        functools.partial(_sc_kernel, tm=tm, grouped=grouped, tiles_per_seq=tiles_per_seq),
        grid=(rows // tm,),
        in_specs=in_specs,
        out_specs=[row_spec(D_MODEL), tail_spec],
        out_shape=[jax.ShapeDtypeStruct((rows, D_MODEL), F32),
                   jax.ShapeDtypeStruct((tail_rows, D_MODEL), F32)],
        scratch_shapes=scratch,
        compiler_params=pltpu.CompilerParams(dimension_semantics=("arbitrary",),
                                             vmem_limit_bytes=VMEM_LIMIT_BYTES),
        name="sc_grouped" if grouped else "sc_seq",
    )(*args)


def _dn_constants(sub, chunk):
    r = jnp.arange(sub)
    same = (r[:, None] // chunk) == (r[None, :] // chunk)
    causal = same & (r[:, None] >= r[None, :])
    strict = same & (r[:, None] > r[None, :])
    lane = jnp.arange(LANES)
    col = jnp.arange(N_HEADS * D_HEAD)
    head_of_col = col // D_HEAD
    return dict(
        ltri=causal.astype(BF16),
        utri=causal.T.astype(BF16),
        negmask=jnp.where(causal, 0.0, -jnp.inf).astype(F32),
        strict=strict.astype(F32),
        eye=jnp.eye(sub, dtype=F32),
        seg=(head_of_col[:, None] == lane[None, :]).astype(BF16),
        exp_b=(lane[:, None] == head_of_col[None, :]).astype(BF16),
        exp_g=(lane[:, None] == head_of_col[None, :] + N_HEADS).astype(BF16),
    )


def _dn_kernel(*refs, tm, sub, chunk, grouped, tiles_per_seq, sdt):
    n_in = 24 if grouped else 22
    ins, rest = refs[:n_in], refs[n_in:]
    if grouped:
        x_ref, hdr_ref, sin_ref = ins[:3]
        ins = ins[3:]
    else:
        x_ref = ins[0]
        ins = ins[1:]
    (wqkv_ref, wz_ref, wba_ref, wbat_ref, cw_ref, alog_c_ref, dtb_c_ref, alog_r_ref, dtb_r_ref,
     ng_ref, wout_ref, g_ref, b_ref,
     ltri_ref, utri_ref, negmask_ref, strict_ref, eye_ref, seg_ref, expb_ref, expg_ref) = ins
    y_ref, tail_ref, sout_ref = rest[:3]
    scr = rest[3:]
    (qkv_s, z_s, ba_s, bat_s, o_s, k_s, kb_s, qs_s, qg_s, vb_s, kbg_s, gcb_s, gcr_s,
     att_s, u_s, w_s, vn_s) = scr[:17]
    if not grouped:
        ext_s, carry_s = scr[17:]

        @pl.when(pl.program_id(0) % tiles_per_seq == 0)
        def _():
            carry_s[...] = jnp.zeros_like(carry_s)
            sout_ref[...] = jnp.zeros_like(sout_ref)

    n_chunks = sub // chunk
    n_levels = int(math.log2(chunk)) - 1

    x = x_ref[...]
    xb = x.astype(BF16)
    for c in range(QKV_DIM // MIX_COLS):
        c0 = c * MIX_COLS
        cols = slice(c0, c0 + MIX_COLS)
        u = jnp.dot(xb, wqkv_ref[:, cols], preferred_element_type=F32)
        if grouped:
            uc = _conv_grouped(u, hdr_ref[:, cols], cols, DN_CONV, cw_ref, tail_ref, SUBLANES)
        else:
            uc = _conv_carry(u, cols, DN_CONV, cw_ref, ext_s, carry_s, tail_ref)
        qkv_s[:, cols] = _silu(uc)
    z_s[...] = jnp.dot(xb, wz_ref[...], preferred_element_type=F32)
    ba_s[...] = jnp.dot(xb, wba_ref[...], preferred_element_type=F32)
    bat_s[...] = lax.dot_general(wbat_ref[...], xb, (((1,), (1,)), ((), ())),
                                 preferred_element_type=F32)

    seg = seg_ref[...]
    expb = expb_ref[...]
    expg = expg_ref[...]
    hd = N_HEADS * D_HEAD

    def sub_tile(si, carry):
        rb = pl.multiple_of(si * sub, sub)
        rows = pl.ds(rb, sub)
        q = qkv_s[rows, 0:hd]
        k = qkv_s[rows, hd:2 * hd]
        v = qkv_s[rows, 2 * hd:3 * hd]
        rq = lax.rsqrt(_dot3(q * q, seg) + L2_EPS)
        rk = lax.rsqrt(_dot3(k * k, seg) + L2_EPS)
        qn = q * _dot3(rq, expb)
        kn = k * _dot3(rk, expb)

        ba = ba_s[rows, :]
        beta_c = jax.nn.sigmoid(ba)
        g_c = -jnp.exp(alog_c_ref[...]) * _softplus(ba + dtb_c_ref[...])
        gc_c = _dot3_l(ltri_ref[...], g_c)
        beta_b = _dot3(beta_c, expb)
        gc_b = _dot3(gc_c, expg)
        eg_b = jnp.exp(gc_b)

        bat = bat_s[:, rows]
        g_r = -jnp.exp(alog_r_ref[:, 0:sub]) * _softplus(bat + dtb_r_ref[:, 0:sub])
        gcr_s[...] = _dot3(g_r, utri_ref[...])

        kbeta = kn * beta_b
        k_s[...] = kn.astype(sdt)
        kb_s[...] = kbeta.astype(sdt)
        qs = qn * (D_HEAD ** -0.5)
        qs_s[...] = qs.astype(sdt)
        qg_s[...] = (qs * eg_b).astype(sdt)
        vb_s[...] = (v * beta_b).astype(sdt)
        kbg_s[...] = (kbeta * eg_b).astype(sdt)
        gcb_s[...] = gc_b

        def head(h, carry_h):
            hs = pl.ds(pl.multiple_of(h * D_HEAD, D_HEAD), D_HEAD)
            k_h = k_s[:, hs]
            gcb = gcb_s[:, hs]
            gcw = gcb[:, 0:sub] if sub <= D_HEAD else jnp.concatenate([gcb] * (sub // D_HEAD), axis=1)
            gcr = gcr_s[pl.ds(N_HEADS + h, 1), :]
            dec = jnp.exp(gcw - gcr + negmask_ref[...])
            a = _bdot_nt(kb_s[:, hs], k_h) * dec * strict_ref[...]
            att_s[...] = (_bdot_nt(qs_s[:, hs], k_h) * dec).astype(sdt)
            bm = -a
            pm = eye_ref[...] + bm
            for _ in range(n_levels):
                b2 = _bdot(bm, bm)
                pm = pm + _bdot(b2, pm)
                bm = b2
            rhs = jnp.concatenate([vb_s[:, hs], kbg_s[:, hs]], axis=1)
            sol = _bdot(pm, rhs)
            u_s[...] = sol[:, 0:D_HEAD]
            w_s[...] = sol[:, D_HEAD:2 * D_HEAD]
            vn_s[...] = jnp.zeros_like(vn_s)

            def chunk_step(c, carry_c):
                r0 = pl.multiple_of(c * chunk, chunk)
                cr = pl.ds(r0, chunk)
                if grouped:
                    s_old = sin_ref[si * n_chunks + c, h]
                else:
                    s_old = sout_ref[h]
                lhs = jnp.concatenate([w_s[cr, :], qg_s[cr, hs].astype(F32)], axis=0)
                ws = _bdot(lhs, s_old)
                v_new = u_s[cr, :] - ws[0:chunk]
                vn_s[cr, :] = v_new.astype(sdt)
                o_c = ws[chunk:2 * chunk] + _bdot(att_s[cr, :], vn_s[...])
                o_s[pl.ds(rb + r0, chunk), hs] = o_c
                gc_rows = gcb_s[cr, hs]
                g_last = gc_rows[chunk - 1:chunk, :]
                kd = k_s[cr, hs].astype(F32) * jnp.exp(g_last - gc_rows)
                s_new = s_old * jnp.exp(g_last) + _bdot_tn(kd, v_new)
                if grouped:
                    sout_ref[si * n_chunks + c, h] = s_new
                else:
                    sout_ref[h] = s_new
                return carry_c

            lax.fori_loop(0, n_chunks, chunk_step, 0)
            return carry_h

        lax.fori_loop(0, N_HEADS, head, 0)
        return carry

    lax.fori_loop(0, tm // sub, sub_tile, 0)

    o = o_s[...]
    ms = _dot3(o * o, seg) * (1.0 / D_HEAD)
    o = o * _dot3(lax.rsqrt(ms + RMS_EPS), expb) * ng_ref[...] * _silu(z_s[...])
    mix = _bdot(o, wout_ref[...])
    y_ref[...] = _layer_norm(ALPHA * x + mix, g_ref[...], b_ref[...])


def _dn_call(x, hdr, s_in, weights, consts, *, tm, sub, chunk, seq_len):
    rows = x.shape[0]
    grouped = hdr is not None
    sdt = BF16 if chunk % (2 * SUBLANES) == 0 else F32
    row_spec = lambda w: pl.BlockSpec((tm, w), lambda i: (i, 0))
    const_list = [consts[k] for k in ("ltri", "utri", "negmask", "strict", "eye", "seg", "exp_b", "exp_g")]
    in_specs = [row_spec(D_MODEL)]
    args = [x]
    hd = N_HEADS * D_HEAD
    if grouped:
        seqs_per_tile = tm // chunk
        s_spec = pl.BlockSpec((seqs_per_tile, N_HEADS, D_HEAD, D_HEAD), lambda i: (i, 0, 0, 0))
        in_specs += [row_spec(QKV_DIM), s_spec]
        args += [hdr, s_in]
        tail_rows, tail_spec = rows, row_spec(QKV_DIM)
        s_out_shape = s_in.shape
        s_out_spec = s_spec
        tiles_per_seq = 1
    else:
        tiles_per_seq = seq_len // tm
        n_seq = rows // seq_len
        tail_rows = n_seq * SUBLANES
        tail_spec = pl.BlockSpec((SUBLANES, QKV_DIM), lambda i: (i // tiles_per_seq, 0))
        s_out_shape = (n_seq, N_HEADS, D_HEAD, D_HEAD)
        s_out_spec = pl.BlockSpec((None, N_HEADS, D_HEAD, D_HEAD), lambda i: (i // tiles_per_seq, 0, 0, 0))
    in_specs += [_const_spec(w.shape) for w in list(weights) + const_list]
    args += list(weights) + const_list
    scratch = [
        pltpu.VMEM((tm, QKV_DIM), F32),
        pltpu.VMEM((tm, hd), F32),
        pltpu.VMEM((tm, LANES), F32),
        pltpu.VMEM((2 * N_HEADS, tm), F32),
        pltpu.VMEM((tm, hd), F32),
        pltpu.VMEM((sub, hd), sdt),
        pltpu.VMEM((sub, hd), sdt),
        pltpu.VMEM((sub, hd), sdt),
        pltpu.VMEM((sub, hd), sdt),
        pltpu.VMEM((sub, hd), sdt),
        pltpu.VMEM((sub, hd), sdt),
        pltpu.VMEM((sub, hd), F32),
        pltpu.VMEM((2 * N_HEADS, sub), F32),
        pltpu.VMEM((sub, sub), sdt),
        pltpu.VMEM((sub, D_HEAD), F32),
        pltpu.VMEM((sub, D_HEAD), F32),
        pltpu.VMEM((sub, D_HEAD), sdt),
    ]
    if not grouped:
        scratch += [pltpu.VMEM((HDR_ROWS + tm, MIX_COLS), F32), pltpu.VMEM((HDR_ROWS, QKV_DIM), F32)]
    return pl.pallas_call(
        functools.partial(_dn_kernel, tm=tm, sub=sub, chunk=chunk, grouped=grouped,
                          tiles_per_seq=tiles_per_seq, sdt=sdt),
        grid=(rows // tm,),
        in_specs=in_specs,
        out_specs=[row_spec(D_MODEL), tail_spec, s_out_spec],
        out_shape=[jax.ShapeDtypeStruct((rows, D_MODEL), F32),
                   jax.ShapeDtypeStruct((tail_rows, QKV_DIM), F32),
                   jax.ShapeDtypeStruct(s_out_shape, F32)],
        scratch_shapes=scratch,
        compiler_params=pltpu.CompilerParams(dimension_semantics=("arbitrary",),
                                             vmem_limit_bytes=VMEM_LIMIT_BYTES),
        name="dn_grouped" if grouped else "dn_seq",
    )(*args)


def _pad_state(state, width):
    b, _, c = state.shape
    return jnp.pad(state, ((0, 0), (SUBLANES - (width - 1), 0), (0, 0))).reshape(b * SUBLANES, c)


def _tail_state(tail, width):
    c = tail.shape[-1]
    return tail.reshape(-1, SUBLANES, c)[:, SUBLANES - (width - 1):, :]


def _row(v):
    return v.reshape(1, -1).astype(F32)


def kernel(x_prompt, x_sample, p_prompt, p_sample, state_dn_conv, state_dn_S, state_sc_conv, state_ffn_conv, dn_w_in, dn_conv_w, dn_a_log, dn_dt_bias, dn_norm_g, dn_w_out, sc_w_in, sc_conv_w, sc_w_out, ffn_w_up, ffn_conv_w, ffn_w_down, ln_mix_g, ln_mix_b, ln_ffn_g, ln_ffn_b, ple_w_proj, ple_w_gate):
    bp, lp, _ = x_prompt.shape
    bs, ls, _ = x_sample.shape
    assert ls == SUBLANES and lp % 512 == 0
    hd = N_HEADS * D_HEAD
    xp = x_prompt.reshape(bp * lp, D_MODEL)
    xs = x_sample.reshape(bs * ls, D_MODEL)
    consts_p = _dn_constants(256, CHUNK)
    consts_s = _dn_constants(64, min(CHUNK, ls))

    outs_p = {"dn_conv": [], "dn_s": [], "sc": [], "ffn": []}
    outs_s = {"dn_conv": [], "dn_s": [], "sc": [], "ffn": []}
    for i in range(DEPTH):
        j = i // 2
        if i % 2 == 0:
            w_in = dn_w_in[j]
            w_ba = w_in[:, QKV_DIM + hd:]
            lane_pad = LANES - 2 * N_HEADS
            zeros8 = jnp.zeros((N_HEADS,), F32)
            alog16 = jnp.concatenate([zeros8, dn_a_log[j].astype(F32)])
            dtb16 = jnp.concatenate([zeros8, dn_dt_bias[j].astype(F32)])
            weights = [
                w_in[:, :QKV_DIM].astype(BF16),
                w_in[:, QKV_DIM:QKV_DIM + hd].astype(BF16),
                jnp.pad(w_ba, ((0, 0), (0, lane_pad))).astype(BF16),
                w_ba.T.astype(BF16),
                dn_conv_w[j].astype(F32),
                jnp.pad(alog16, (0, lane_pad)).reshape(1, LANES),
                jnp.pad(dtb16, (0, lane_pad)).reshape(1, LANES),
                jnp.broadcast_to(alog16[:, None], (2 * N_HEADS, 256)),
                jnp.broadcast_to(dtb16[:, None], (2 * N_HEADS, 256)),
                _row(jnp.tile(dn_norm_g[j], N_HEADS)),
                dn_w_out[j].astype(BF16),
                _row(ln_mix_g[i]), _row(ln_mix_b[i]),
            ]
            xp, tail, s_out = _dn_call(xp, None, None, weights, consts_p, tm=512, sub=256, chunk=CHUNK, seq_len=lp)
            outs_p["dn_conv"].append(_tail_state(tail, DN_CONV))
            outs_p["dn_s"].append(s_out)
            xs, tail, s_out = _dn_call(xs, _pad_state(state_dn_conv[j], DN_CONV), state_dn_S[j], weights, consts_s,
                                       tm=64, sub=64, chunk=ls, seq_len=ls)
            outs_s["dn_conv"].append(_tail_state(tail, DN_CONV))
            outs_s["dn_s"].append(s_out)
        else:
            weights = [sc_w_in[j].astype(BF16), sc_conv_w[j].astype(F32), sc_w_out[j].astype(BF16),
                       _row(ln_mix_g[i]), _row(ln_mix_b[i])]
            xp, tail = _sc_call(xp, None, *weights, tm=512, seq_len=lp)
            outs_p["sc"].append(_tail_state(tail, SC_CONV))
            xs, tail = _sc_call(xs, _pad_state(state_sc_conv[j], SC_CONV), *weights, tm=512, seq_len=ls)
            outs_s["sc"].append(_tail_state(tail, SC_CONV))
        weights = [ffn_w_up[i].astype(BF16), ffn_conv_w[i].astype(F32), ffn_w_down[i].astype(BF16),
                   _row(ln_ffn_g[i]), _row(ln_ffn_b[i]), ple_w_proj[i].astype(BF16), ple_w_gate[i].astype(BF16)]
        xp, tail = _ffn_call(xp, None, p_prompt[i].reshape(bp * lp, D_PLE), *weights, tm=512, seq_len=lp)
        outs_p["ffn"].append(_tail_state(tail, FFN_CONV))
        xs, tail = _ffn_call(xs, _pad_state(state_ffn_conv[i], FFN_CONV), p_sample[i].reshape(bs * ls, D_PLE),
                             *weights, tm=128, seq_len=ls)
        outs_s["ffn"].append(_tail_state(tail, FFN_CONV))

    return (xp.reshape(bp, lp, D_MODEL), xs.reshape(bs, ls, D_MODEL),
            jnp.stack(outs_p["dn_conv"]), jnp.stack(outs_p["dn_s"]), jnp.stack(outs_p["sc"]), jnp.stack(outs_p["ffn"]),
            jnp.stack(outs_s["dn_conv"]), jnp.stack(outs_s["dn_s"]), jnp.stack(outs_s["sc"]), jnp.stack(outs_s["ffn"]))
```

```python
import functools
import math

import jax
import jax.numpy as jnp
from jax import lax
from jax.experimental import pallas as pl
from jax.experimental.pallas import tpu as pltpu

F32 = jnp.float32
BF16 = jnp.bfloat16

D_MODEL = 1024
DEPTH = 4
N_HEADS = 8
D_HEAD = 128
QKV_DIM = 3 * N_HEADS * D_HEAD
DN_CONV = 4
CHUNK = 64
SC_CONV = 3
D_FF = 2816
FFN_CONV = 3
D_PLE = 256
ALPHA = (2.0 * DEPTH) ** 0.25
LN_EPS = 1e-5
RMS_EPS = 1e-6
L2_EPS = 1e-6

LANES = 128
SUBLANES = 8
VMEM_LIMIT_BYTES = 56 * 1024 * 1024
HDR_ROWS = 16
FFN_COLS = 256
MIX_COLS = 512


def _bdot(a, b):
    return jnp.dot(a.astype(BF16), b.astype(BF16), preferred_element_type=F32)


def _bdot_nt(a, b):
    return lax.dot_general(a.astype(BF16), b.astype(BF16), (((1,), (1,)), ((), ())),
                           preferred_element_type=F32)


def _bdot_tn(a, b):
    return lax.dot_general(a.astype(BF16), b.astype(BF16), (((0,), (0,)), ((), ())),
                           preferred_element_type=F32)


def _split3(x):
    hi = x.astype(BF16)
    r = x - hi.astype(F32)
    mid = r.astype(BF16)
    lo = (r - mid.astype(F32)).astype(BF16)
    return hi, mid, lo


def _dot3(x, e):
    hi, mid, lo = _split3(x)
    return (jnp.dot(hi, e, preferred_element_type=F32) + jnp.dot(mid, e, preferred_element_type=F32)
            + jnp.dot(lo, e, preferred_element_type=F32))


def _dot2(x, e):
    hi = x.astype(BF16)
    lo = (x - hi.astype(F32)).astype(BF16)
    return jnp.dot(hi, e, preferred_element_type=F32) + jnp.dot(lo, e, preferred_element_type=F32)


def _dot3_l(e, x):
    hi, mid, lo = _split3(x)
    return (jnp.dot(e, hi, preferred_element_type=F32) + jnp.dot(e, mid, preferred_element_type=F32)
            + jnp.dot(e, lo, preferred_element_type=F32))


def _layer_norm(x, g, b):
    mu = jnp.mean(x, axis=-1, keepdims=True)
    xc = x - mu
    var = jnp.mean(xc * xc, axis=-1, keepdims=True)
    return xc * lax.rsqrt(var + LN_EPS) * g + b


def _silu(x):
    return x * jax.nn.sigmoid(x)


def _softplus(x):
    return jnp.maximum(x, 0.0) + jnp.log1p(jnp.exp(-jnp.abs(x)))


def _conv_carry(u, cols, width, cw_ref, ext_ref, carry_ref, tail_ref):
    tm = u.shape[0]
    ext_ref[0:HDR_ROWS, :] = carry_ref[:, cols]
    ext_ref[HDR_ROWS:HDR_ROWS + tm, :] = u
    y = None
    for j in range(width):
        d = width - 1 - j
        src = u if d == 0 else ext_ref[HDR_ROWS - d:HDR_ROWS - d + tm, :]
        term = src * cw_ref[j:j + 1, cols]
        y = term if y is None else y + term
    carry_ref[:, cols] = u[tm - HDR_ROWS:tm, :]
    tail_ref[:, cols] = u[tm - SUBLANES:tm, :]
    return y


def _conv_grouped(u, hdr, cols, width, cw_ref, tail_ref, period):
    n = u.shape[0]
    assert period & (period - 1) == 0
    t = lax.broadcasted_iota(jnp.int32, u.shape, 0) & (period - 1)
    y = None
    for j in range(width):
        d = width - 1 - j
        if d == 0:
            src = u
        else:
            src = jnp.where(t >= d, pltpu.roll(u, d, 0), pltpu.roll(hdr, (n - period + d) % n, 0))
        term = src * cw_ref[j:j + 1, cols]
        y = term if y is None else y + term
    tail_ref[:, cols] = u
    return y


def _residual_ln_ple(x, f, p, g_ref, b_ref, wproj_ref, wgate_ref):
    xn = _layer_norm(ALPHA * x + f, g_ref[...], b_ref[...])
    gate = jax.nn.sigmoid(_bdot(xn, wgate_ref[...]))
    return xn + _bdot(p, wproj_ref[...]) * gate


def _ffn_kernel(*refs, tm, grouped, tiles_per_seq):
    if grouped:
        (x_ref, hdr_ref, p_ref, wup_ref, cw_ref, wdn_ref, g_ref, b_ref, wproj_ref, wgate_ref,
         y_ref, tail_ref, h_s) = refs
    else:
        (x_ref, p_ref, wup_ref, cw_ref, wdn_ref, g_ref, b_ref, wproj_ref, wgate_ref,
         y_ref, tail_ref, h_s, ext_s, carry_s) = refs

        @pl.when(pl.program_id(0) % tiles_per_seq == 0)
        def _():
            carry_s[...] = jnp.zeros_like(carry_s)

    x = x_ref[...]
    xb = x.astype(BF16)

    def conv(u, c0):
        cols = slice(c0, c0 + FFN_COLS)
        if grouped:
            return _conv_grouped(u, hdr_ref[:, cols], cols, FFN_CONV, cw_ref, tail_ref, SUBLANES)
        return _conv_carry(u, cols, FFN_CONV, cw_ref, ext_s, carry_s, tail_ref)

    for c in range(D_FF // FFN_COLS):
        c0 = c * FFN_COLS
        val = conv(jnp.dot(xb, wup_ref[:, c0:c0 + FFN_COLS], preferred_element_type=F32), c0)
        gat = conv(jnp.dot(xb, wup_ref[:, D_FF + c0:D_FF + c0 + FFN_COLS], preferred_element_type=F32),
                   D_FF + c0)
        h_s[:, c0:c0 + FFN_COLS] = (_silu(gat) * val).astype(BF16)

    f = jnp.dot(h_s[...], wdn_ref[...], preferred_element_type=F32)
    y_ref[...] = _residual_ln_ple(x, f, p_ref[...], g_ref, b_ref, wproj_ref, wgate_ref)


def _const_spec(shape):
    zeros = (0,) * len(shape)
    return pl.BlockSpec(shape, lambda i: zeros, pipeline_mode=pl.Buffered(1))


def _ffn_call(x, hdr, p, wup, cw, wdn, g, b, wproj, wgate, *, tm, seq_len):
    rows = x.shape[0]
    grouped = hdr is not None
    row_spec = lambda w: pl.BlockSpec((tm, w), lambda i: (i, 0))
    weights = [wup, cw, wdn, g, b, wproj, wgate]
    in_specs = [row_spec(D_MODEL)] + ([row_spec(2 * D_FF)] if grouped else []) + [row_spec(D_PLE)]
    in_specs += [_const_spec(w.shape) for w in weights]
    args = [x] + ([hdr] if grouped else []) + [p] + weights
    scratch = [pltpu.VMEM((tm, D_FF), BF16)]
    if grouped:
        tail_rows = rows
        tail_spec = row_spec(2 * D_FF)
        tiles_per_seq = 1
    else:
        tiles_per_seq = seq_len // tm
        tail_rows = rows // seq_len * SUBLANES
        tail_spec = pl.BlockSpec((SUBLANES, 2 * D_FF), lambda i: (i // tiles_per_seq, 0))
        scratch += [pltpu.VMEM((HDR_ROWS + tm, FFN_COLS), F32), pltpu.VMEM((HDR_ROWS, 2 * D_FF), F32)]
    return pl.pallas_call(
        functools.partial(_ffn_kernel, tm=tm, grouped=grouped, tiles_per_seq=tiles_per_seq),
        grid=(rows // tm,),
        in_specs=in_specs,
        out_specs=[row_spec(D_MODEL), tail_spec],
        out_shape=[jax.ShapeDtypeStruct((rows, D_MODEL), F32),
                   jax.ShapeDtypeStruct((tail_rows, 2 * D_FF), F32)],
        scratch_shapes=scratch,
        compiler_params=pltpu.CompilerParams(dimension_semantics=("arbitrary",),
                                             vmem_limit_bytes=VMEM_LIMIT_BYTES),
        name="ffn_grouped" if grouped else "ffn_seq",
    )(*args)


def _sc_kernel(*refs, tm, grouped, tiles_per_seq):
    if grouped:
        x_ref, hdr_ref, win_ref, cw_ref, wout_ref, g_ref, b_ref, y_ref, tail_ref, h_s = refs
    else:
        x_ref, win_ref, cw_ref, wout_ref, g_ref, b_ref, y_ref, tail_ref, h_s, ext_s, carry_s = refs

        @pl.when(pl.program_id(0) % tiles_per_seq == 0)
        def _():
            carry_s[...] = jnp.zeros_like(carry_s)

    x = x_ref[...]
    xb = x.astype(BF16)
    for c in range(D_MODEL // MIX_COLS):
        c0 = c * MIX_COLS
        cols = slice(c0, c0 + MIX_COLS)
        bb = jnp.dot(xb, win_ref[:, c0:c0 + MIX_COLS], preferred_element_type=F32)
        cc = jnp.dot(xb, win_ref[:, D_MODEL + c0:D_MODEL + c0 + MIX_COLS], preferred_element_type=F32)
        hh = jnp.dot(xb, win_ref[:, 2 * D_MODEL + c0:2 * D_MODEL + c0 + MIX_COLS], preferred_element_type=F32)
        u = cc * hh
        if grouped:
            uc = _conv_grouped(u, hdr_ref[:, cols], cols, SC_CONV, cw_ref, tail_ref, SUBLANES)
        else:
            uc = _conv_carry(u, cols, SC_CONV, cw_ref, ext_s, carry_s, tail_ref)
        h_s[:, cols] = (bb * uc).astype(BF16)
    mix = jnp.dot(h_s[...], wout_ref[...], preferred_element_type=F32)
    y_ref[...] = _layer_norm(ALPHA * x + mix, g_ref[...], b_ref[...])


def _sc_call(x, hdr, win, cw, wout, g, b, *, tm, seq_len):
    rows = x.shape[0]
    grouped = hdr is not None
    row_spec = lambda w: pl.BlockSpec((tm, w), lambda i: (i, 0))
    weights = [win, cw, wout, g, b]
    in_specs = [row_spec(D_MODEL)] + ([row_spec(D_MODEL)] if grouped else [])
    in_specs += [_const_spec(w.shape) for w in weights]
    args = [x] + ([hdr] if grouped else []) + weights
    scratch = [pltpu.VMEM((tm, D_MODEL), BF16)]
    if grouped:
        tail_rows = rows
        tail_spec = row_spec(D_MODEL)
        tiles_per_seq = 1
    else:
        tiles_per_seq = seq_len // tm
        tail_rows = rows // seq_len * SUBLANES
        tail_spec = pl.BlockSpec((SUBLANES, D_MODEL), lambda i: (i // tiles_per_seq, 0))
        scratch += [pltpu.VMEM((HDR_ROWS + tm, MIX_COLS), F32), pltpu.VMEM((HDR_ROWS, D_MODEL), F32)]
    return pl.pallas_call(
        functools.partial(_sc_kernel, tm=tm, grouped=grouped, tiles_per_seq=tiles_per_seq),
        grid=(rows // tm,),
        in_specs=in_specs,
        out_specs=[row_spec(D_MODEL), tail_spec],
        out_shape=[jax.ShapeDtypeStruct((rows, D_MODEL), F32),
                   jax.ShapeDtypeStruct((tail_rows, D_MODEL), F32)],
        scratch_shapes=scratch,
        compiler_params=pltpu.CompilerParams(dimension_semantics=("arbitrary",),
                                             vmem_limit_bytes=VMEM_LIMIT_BYTES),
        name="sc_grouped" if grouped else "sc_seq",
    )(*args)


def _dn_constants(sub, chunk):
    r = jnp.arange(sub)
    same = (r[:, None] // chunk) == (r[None, :] // chunk)
    causal = same & (r[:, None] >= r[None, :])
    strict = same & (r[:, None] > r[None, :])
    lane = jnp.arange(LANES)
    col = jnp.arange(N_HEADS * D_HEAD)
    head_of_col = col // D_HEAD
    return dict(
        ltri=causal.astype(BF16),
        utri=causal.T.astype(BF16),
        negmask=jnp.where(causal, 0.0, -jnp.inf).astype(F32),
        strict=strict.astype(F32),
        eye=jnp.eye(sub, dtype=F32),
        seg=(head_of_col[:, None] == lane[None, :]).astype(BF16),
        exp_b=(lane[:, None] == head_of_col[None, :]).astype(BF16),
        exp_g=(lane[:, None] == head_of_col[None, :] + N_HEADS).astype(BF16),
    )


def _dn_kernel(*refs, tm, sub, chunk, grouped, tiles_per_seq, sdt):
    n_in = 24 if grouped else 22
    ins, rest = refs[:n_in], refs[n_in:]
    if grouped:
        x_ref, hdr_ref, sin_ref = ins[:3]
        ins = ins[3:]
    else:
        x_ref = ins[0]
        ins = ins[1:]
    (wqkv_ref, wz_ref, wba_ref, wbat_ref, cw_ref, alog_c_ref, dtb_c_ref, alog_r_ref, dtb_r_ref,
     ng_ref, wout_ref, g_ref, b_ref,
     ltri_ref, utri_ref, negmask_ref, strict_ref, eye_ref, seg_ref, expb_ref, expg_ref) = ins
    y_ref, tail_ref, sout_ref = rest[:3]
    scr = rest[3:]
    (qkv_s, z_s, ba_s, bat_s, o_s, k_s, kb_s, qs_s, qg_s, vb_s, kbg_s, gcb_s, gcr_s,
     att_s, u_s, w_s, vn_s) = scr[:17]
    if not grouped:
        ext_s, carry_s = scr[17:]

        @pl.when(pl.program_id(0) % tiles_per_seq == 0)
        def _():
            carry_s[...] = jnp.zeros_like(carry_s)
            sout_ref[...] = jnp.zeros_like(sout_ref)

    n_chunks = sub // chunk
    n_levels = int(math.log2(chunk)) - 1

    x = x_ref[...]
    xb = x.astype(BF16)
    for c in range(QKV_DIM // MIX_COLS):
        c0 = c * MIX_COLS
        cols = slice(c0, c0 + MIX_COLS)
        u = jnp.dot(xb, wqkv_ref[:, cols], preferred_element_type=F32)
        if grouped:
            uc = _conv_grouped(u, hdr_ref[:, cols], cols, DN_CONV, cw_ref, tail_ref, SUBLANES)
        else:
            uc = _conv_carry(u, cols, DN_CONV, cw_ref, ext_s, carry_s, tail_ref)
        qkv_s[:, cols] = _silu(uc)
    z_s[...] = jnp.dot(xb, wz_ref[...], preferred_element_type=F32)
    ba_s[...] = jnp.dot(xb, wba_ref[...], preferred_element_type=F32)
    bat_s[...] = lax.dot_general(wbat_ref[...], xb, (((1,), (1,)), ((), ())),
                                 preferred_element_type=F32)

    seg = seg_ref[...]
    expb = expb_ref[...]
    expg = expg_ref[...]
    hd = N_HEADS * D_HEAD

    def sub_tile(si, carry):
        rb = pl.multiple_of(si * sub, sub)
        rows = pl.ds(rb, sub)
        q = qkv_s[rows, 0:hd]
        k = qkv_s[rows, hd:2 * hd]
        v = qkv_s[rows, 2 * hd:3 * hd]
        rq = lax.rsqrt(_dot2(q * q, seg) + L2_EPS)
        rk = lax.rsqrt(_dot2(k * k, seg) + L2_EPS)
        qn = q * _dot2(rq, expb)
        kn = k * _dot2(rk, expb)

        ba = ba_s[rows, :]
        beta_c = jax.nn.sigmoid(ba)
        g_c = -jnp.exp(alog_c_ref[...]) * _softplus(ba + dtb_c_ref[...])
        gc_c = _dot3_l(ltri_ref[...], g_c)
        beta_b = _dot2(beta_c, expb)
        gc_b = _dot3(gc_c, expg)
        eg_b = jnp.exp(gc_b)

        bat = bat_s[:, rows]
        g_r = -jnp.exp(alog_r_ref[:, 0:sub]) * _softplus(bat + dtb_r_ref[:, 0:sub])
        gcr_s[...] = _dot3(g_r, utri_ref[...])

        kbeta = kn * beta_b
        k_s[...] = kn.astype(sdt)
        kb_s[...] = kbeta.astype(sdt)
        qs = qn * (D_HEAD ** -0.5)
        qs_s[...] = qs.astype(sdt)
        qg_s[...] = (qs * eg_b).astype(sdt)
        vb_s[...] = (v * beta_b).astype(sdt)
        kbg_s[...] = (kbeta * eg_b).astype(sdt)
        gcb_s[...] = gc_b

        negmask = negmask_ref[...]
        strict = strict_ref[...]
        eye = eye_ref[...]
        heads = range(N_HEADS)
        hsl = [slice(h * D_HEAD, (h + 1) * D_HEAD) for h in heads]
        kk = [_bdot_nt(kb_s[:, hsl[h]], k_s[:, hsl[h]]) for h in heads]
        qk = [_bdot_nt(qs_s[:, hsl[h]], k_s[:, hsl[h]]) for h in heads]
        bm, pm = [], []
        for h in heads:
            gcb = gcb_s[:, hsl[h]]
            gcw = gcb[:, 0:sub] if sub <= D_HEAD else jnp.concatenate([gcb] * (sub // D_HEAD), axis=1)
            gcr = gcr_s[N_HEADS + h:N_HEADS + h + 1, :]
            dec = jnp.exp(gcw - gcr + negmask)
            att_s[h] = (qk[h] * dec).astype(sdt)
            b0 = -(kk[h] * dec * strict)
            bm.append(b0)
            pm.append(eye + b0)
        bm = [_bdot(b, b) for b in bm]
        for _ in range(1, n_levels):
            r = [_bdot(bm[h], jnp.concatenate([bm[h], pm[h]], axis=1)) for h in heads]
            bm = [r[h][:, 0:sub] for h in heads]
            pm = [pm[h] + r[h][:, sub:2 * sub] for h in heads]
        pm = [pm[h] + _bdot(bm[h], pm[h]) for h in heads]
        sol = [_bdot(pm[h], jnp.concatenate([vb_s[:, hsl[h]], kbg_s[:, hsl[h]]], axis=1)) for h in heads]
        for h in heads:
            u_s[:, hsl[h]] = sol[h][:, 0:D_HEAD]
            w_s[:, hsl[h]] = sol[h][:, D_HEAD:2 * D_HEAD]
        vn_s[...] = jnp.zeros_like(vn_s)

        def chunk_step(c, carry_c):
            r0 = pl.multiple_of(c * chunk, chunk)
            cr = pl.ds(r0, chunk)
            if grouped:
                s_old = [sin_ref[si * n_chunks + c, h] for h in heads]
            else:
                s_old = [sout_ref[h] for h in heads]
            ws = [_bdot(jnp.concatenate([w_s[cr, hsl[h]], qg_s[cr, hsl[h]].astype(F32)], axis=0), s_old[h])
                  for h in heads]
            v_new = [u_s[cr, hsl[h]] - ws[h][0:chunk] for h in heads]
            for h in heads:
                vn_s[cr, hsl[h]] = v_new[h].astype(sdt)
            for h in heads:
                o_s[pl.ds(rb + r0, chunk), hsl[h]] = (ws[h][chunk:2 * chunk]
                                                      + _bdot(att_s[h, cr, :], vn_s[:, hsl[h]]))
            for h in heads:
                gc_rows = gcb_s[cr, hsl[h]]
                g_last = gc_rows[chunk - 1:chunk, :]
                kd = k_s[cr, hsl[h]].astype(F32) * jnp.exp(g_last - gc_rows)
                s_new = s_old[h] * jnp.exp(g_last) + _bdot_tn(kd, v_new[h])
                if grouped:
                    sout_ref[si * n_chunks + c, h] = s_new
                else:
                    sout_ref[h] = s_new
            return carry_c

        lax.fori_loop(0, n_chunks, chunk_step, 0)
        return carry

    lax.fori_loop(0, tm // sub, sub_tile, 0)

    o = o_s[...]
    ms = _dot2(o * o, seg) * (1.0 / D_HEAD)
    o = o * _dot2(lax.rsqrt(ms + RMS_EPS), expb) * ng_ref[...] * _silu(z_s[...])
    mix = _bdot(o, wout_ref[...])
    y_ref[...] = _layer_norm(ALPHA * x + mix, g_ref[...], b_ref[...])


def _dn_call(x, hdr, s_in, weights, consts, *, tm, sub, chunk, seq_len):
    rows = x.shape[0]
    grouped = hdr is not None
    sdt = BF16 if chunk % (2 * SUBLANES) == 0 else F32
    row_spec = lambda w: pl.BlockSpec((tm, w), lambda i: (i, 0))
    const_list = [consts[k] for k in ("ltri", "utri", "negmask", "strict", "eye", "seg", "exp_b", "exp_g")]
    in_specs = [row_spec(D_MODEL)]
    args = [x]
    hd = N_HEADS * D_HEAD
    if grouped:
        seqs_per_tile = tm // chunk
        s_spec = pl.BlockSpec((seqs_per_tile, N_HEADS, D_HEAD, D_HEAD), lambda i: (i, 0, 0, 0))
        in_specs += [row_spec(QKV_DIM), s_spec]
        args += [hdr, s_in]
        tail_rows, tail_spec = rows, row_spec(QKV_DIM)
        s_out_shape = s_in.shape
        s_out_spec = s_spec
        tiles_per_seq = 1
    else:
        tiles_per_seq = seq_len // tm
        n_seq = rows // seq_len
        tail_rows = n_seq * SUBLANES
        tail_spec = pl.BlockSpec((SUBLANES, QKV_DIM), lambda i: (i // tiles_per_seq, 0))
        s_out_shape = (n_seq, N_HEADS, D_HEAD, D_HEAD)
        s_out_spec = pl.BlockSpec((None, N_HEADS, D_HEAD, D_HEAD), lambda i: (i // tiles_per_seq, 0, 0, 0))
    in_specs += [_const_spec(w.shape) for w in list(weights) + const_list]
    args += list(weights) + const_list
    scratch = [
        pltpu.VMEM((tm, QKV_DIM), F32),
        pltpu.VMEM((tm, hd), F32),
        pltpu.VMEM((tm, LANES), F32),
        pltpu.VMEM((2 * N_HEADS, tm), F32),
        pltpu.VMEM((tm, hd), F32),
        pltpu.VMEM((sub, hd), sdt),
        pltpu.VMEM((sub, hd), sdt),
        pltpu.VMEM((sub, hd), sdt),
        pltpu.VMEM((sub, hd), sdt),
        pltpu.VMEM((sub, hd), sdt),
        pltpu.VMEM((sub, hd), sdt),
        pltpu.VMEM((sub, hd), F32),
        pltpu.VMEM((2 * N_HEADS, sub), F32),
        pltpu.VMEM((N_HEADS, sub, sub), sdt),
        pltpu.VMEM((sub, hd), F32),
        pltpu.VMEM((sub, hd), F32),
        pltpu.VMEM((sub, hd), sdt),
    ]
    if not grouped:
        scratch += [pltpu.VMEM((HDR_ROWS + tm, MIX_COLS), F32), pltpu.VMEM((HDR_ROWS, QKV_DIM), F32)]
    return pl.pallas_call(
        functools.partial(_dn_kernel, tm=tm, sub=sub, chunk=chunk, grouped=grouped,
                          tiles_per_seq=tiles_per_seq, sdt=sdt),
        grid=(rows // tm,),
        in_specs=in_specs,
        out_specs=[row_spec(D_MODEL), tail_spec, s_out_spec],
        out_shape=[jax.ShapeDtypeStruct((rows, D_MODEL), F32),
                   jax.ShapeDtypeStruct((tail_rows, QKV_DIM), F32),
                   jax.ShapeDtypeStruct(s_out_shape, F32)],
        scratch_shapes=scratch,
        compiler_params=pltpu.CompilerParams(dimension_semantics=("arbitrary",),
                                             vmem_limit_bytes=VMEM_LIMIT_BYTES),
        name="dn_grouped" if grouped else "dn_seq",
    )(*args)


def _pad_state(state, width):
    b, _, c = state.shape
    return jnp.pad(state, ((0, 0), (SUBLANES - (width - 1), 0), (0, 0))).reshape(b * SUBLANES, c)


def _tail_state(tail, width):
    c = tail.shape[-1]
    return tail.reshape(-1, SUBLANES, c)[:, SUBLANES - (width - 1):, :]


def _row(v):
    return v.reshape(1, -1).astype(F32)


def kernel(x_prompt, x_sample, p_prompt, p_sample, state_dn_conv, state_dn_S, state_sc_conv, state_ffn_conv, dn_w_in, dn_conv_w, dn_a_log, dn_dt_bias, dn_norm_g, dn_w_out, sc_w_in, sc_conv_w, sc_w_out, ffn_w_up, ffn_conv_w, ffn_w_down, ln_mix_g, ln_mix_b, ln_ffn_g, ln_ffn_b, ple_w_proj, ple_w_gate):
    bp, lp, _ = x_prompt.shape
    bs, ls, _ = x_sample.shape
    assert ls == SUBLANES and lp % 512 == 0
    hd = N_HEADS * D_HEAD
    xp = x_prompt.reshape(bp * lp, D_MODEL)
    xs = x_sample.reshape(bs * ls, D_MODEL)
    consts_p = _dn_constants(128, CHUNK)
    consts_s = _dn_constants(64, min(CHUNK, ls))

    outs_p = {"dn_conv": [], "dn_s": [], "sc": [], "ffn": []}
    outs_s = {"dn_conv": [], "dn_s": [], "sc": [], "ffn": []}
    for i in range(DEPTH):
        j = i // 2
        if i % 2 == 0:
            w_in = dn_w_in[j]
            w_ba = w_in[:, QKV_DIM + hd:]
            lane_pad = LANES - 2 * N_HEADS
            zeros8 = jnp.zeros((N_HEADS,), F32)
            alog16 = jnp.concatenate([zeros8, dn_a_log[j].astype(F32)])
            dtb16 = jnp.concatenate([zeros8, dn_dt_bias[j].astype(F32)])
            weights = [
                w_in[:, :QKV_DIM].astype(BF16),
                w_in[:, QKV_DIM:QKV_DIM + hd].astype(BF16),
                jnp.pad(w_ba, ((0, 0), (0, lane_pad))).astype(BF16),
                w_ba.T.astype(BF16),
                dn_conv_w[j].astype(F32),
                jnp.pad(alog16, (0, lane_pad)).reshape(1, LANES),
                jnp.pad(dtb16, (0, lane_pad)).reshape(1, LANES),
                jnp.broadcast_to(alog16[:, None], (2 * N_HEADS, 256)),
                jnp.broadcast_to(dtb16[:, None], (2 * N_HEADS, 256)),
                _row(jnp.tile(dn_norm_g[j], N_HEADS)),
                dn_w_out[j].astype(BF16),
                _row(ln_mix_g[i]), _row(ln_mix_b[i]),
            ]
            xp, tail, s_out = _dn_call(xp, None, None, weights, consts_p, tm=512, sub=128, chunk=CHUNK, seq_len=lp)
            outs_p["dn_conv"].append(_tail_state(tail, DN_CONV))
            outs_p["dn_s"].append(s_out)
            xs, tail, s_out = _dn_call(xs, _pad_state(state_dn_conv[j], DN_CONV), state_dn_S[j], weights, consts_s,
                                       tm=64, sub=64, chunk=ls, seq_len=ls)
            outs_s["dn_conv"].append(_tail_state(tail, DN_CONV))
            outs_s["dn_s"].append(s_out)
        else:
            weights = [sc_w_in[j].astype(BF16), sc_conv_w[j].astype(F32), sc_w_out[j].astype(BF16),
                       _row(ln_mix_g[i]), _row(ln_mix_b[i])]
            xp, tail = _sc_call(xp, None, *weights, tm=512, seq_len=lp)
            outs_p["sc"].append(_tail_state(tail, SC_CONV))
            xs, tail = _sc_call(xs, _pad_state(state_sc_conv[j], SC_CONV), *weights, tm=512, seq_len=ls)
            outs_s["sc"].append(_tail_state(tail, SC_CONV))
        weights = [ffn_w_up[i].astype(BF16), ffn_conv_w[i].astype(F32), ffn_w_down[i].astype(BF16),
                   _row(ln_ffn_g[i]), _row(ln_ffn_b[i]), ple_w_proj[i].astype(BF16), ple_w_gate[i].astype(BF16)]
        xp, tail = _ffn_call(xp, None, p_prompt[i].reshape(bp * lp, D_PLE), *weights, tm=512, seq_len=lp)
        outs_p["ffn"].append(_tail_state(tail, FFN_CONV))
        xs, tail = _ffn_call(xs, _pad_state(state_ffn_conv[i], FFN_CONV), p_sample[i].reshape(bs * ls, D_PLE),
                             *weights, tm=128, seq_len=ls)
        outs_s["ffn"].append(_tail_state(tail, FFN_CONV))

    return (xp.reshape(bp, lp, D_MODEL), xs.reshape(bs, ls, D_MODEL),
            jnp.stack(outs_p["dn_conv"]), jnp.stack(outs_p["dn_s"]), jnp.stack(outs_p["sc"]), jnp.stack(outs_p["ffn"]),
            jnp.stack(outs_s["dn_conv"]), jnp.stack(outs_s["dn_s"]), jnp.stack(outs_s["sc"]), jnp.stack(outs_s["ffn"]))
```

```python
import functools
import math

import jax
import jax.numpy as jnp
from jax import lax
from jax.experimental import pallas as pl
from jax.experimental.pallas import tpu as pltpu

F32 = jnp.float32
BF16 = jnp.bfloat16

D_MODEL = 1024
DEPTH = 4
N_HEADS = 8
D_HEAD = 128
QKV_DIM = 3 * N_HEADS * D_HEAD
DN_CONV = 4
CHUNK = 64
SC_CONV = 3
D_FF = 2816
FFN_CONV = 3
D_PLE = 256
ALPHA = (2.0 * DEPTH) ** 0.25
LN_EPS = 1e-5
RMS_EPS = 1e-6
L2_EPS = 1e-6

LANES = 128
SUBLANES = 8
VMEM_LIMIT_BYTES = 56 * 1024 * 1024
HDR_ROWS = 16
FFN_COLS = 256
MIX_COLS = 512


def _bdot(a, b):
    return jnp.dot(a.astype(BF16), b.astype(BF16), preferred_element_type=F32)


def _bdot_nt(a, b):
    return lax.dot_general(a.astype(BF16), b.astype(BF16), (((1,), (1,)), ((), ())),
                           preferred_element_type=F32)


def _bdot_tn(a, b):
    return lax.dot_general(a.astype(BF16), b.astype(BF16), (((0,), (0,)), ((), ())),
                           preferred_element_type=F32)


def _split3(x):
    hi = x.astype(BF16)
    r = x - hi.astype(F32)
    mid = r.astype(BF16)
    lo = (r - mid.astype(F32)).astype(BF16)
    return hi, mid, lo


def _dot3(x, e):
    hi, mid, lo = _split3(x)
    return (jnp.dot(hi, e, preferred_element_type=F32) + jnp.dot(mid, e, preferred_element_type=F32)
            + jnp.dot(lo, e, preferred_element_type=F32))


def _dot2(x, e):
    hi = x.astype(BF16)
    lo = (x - hi.astype(F32)).astype(BF16)
    return jnp.dot(hi, e, preferred_element_type=F32) + jnp.dot(lo, e, preferred_element_type=F32)


def _dot3_l(e, x):
    hi, mid, lo = _split3(x)
    return (jnp.dot(e, hi, preferred_element_type=F32) + jnp.dot(e, mid, preferred_element_type=F32)
            + jnp.dot(e, lo, preferred_element_type=F32))


def _layer_norm(x, g, b):
    mu = jnp.mean(x, axis=-1, keepdims=True)
    xc = x - mu
    var = jnp.mean(xc * xc, axis=-1, keepdims=True)
    return xc * lax.rsqrt(var + LN_EPS) * g + b


def _silu(x):
    return x * jax.nn.sigmoid(x)


def _softplus(x):
    return jnp.maximum(x, 0.0) + jnp.log1p(jnp.exp(-jnp.abs(x)))


def _conv_carry(u, cols, width, cw_ref, ext_ref, carry_ref, tail_ref):
    tm = u.shape[0]
    del ext_ref
    head = u[0:HDR_ROWS, :]
    prev = carry_ref[:, cols]
    t = lax.broadcasted_iota(jnp.int32, head.shape, 0)
    y = None
    yh = None
    for j in range(width):
        d = width - 1 - j
        w = cw_ref[j:j + 1, cols]
        term = (u if d == 0 else pltpu.roll(u, d, 0)) * w
        y = term if y is None else y + term
        src = head if d == 0 else jnp.where(t >= d, pltpu.roll(head, d, 0), pltpu.roll(prev, d, 0))
        term = src * w
        yh = term if yh is None else yh + term
    carry_ref[:, cols] = u[tm - HDR_ROWS:tm, :]
    tail_ref[:, cols] = u[tm - SUBLANES:tm, :]
    return jnp.concatenate([yh, y[HDR_ROWS:tm, :]], axis=0)


def _conv_grouped(u, hdr, cols, width, cw_ref, tail_ref, period):
    n = u.shape[0]
    assert period & (period - 1) == 0
    t = lax.broadcasted_iota(jnp.int32, u.shape, 0) & (period - 1)
    y = None
    for j in range(width):
        d = width - 1 - j
        if d == 0:
            src = u
        else:
            src = jnp.where(t >= d, pltpu.roll(u, d, 0), pltpu.roll(hdr, (n - period + d) % n, 0))
        term = src * cw_ref[j:j + 1, cols]
        y = term if y is None else y + term
    tail_ref[:, cols] = u
    return y


def _residual_ln_ple(x, f, p, g_ref, b_ref, wproj_ref, wgate_ref):
    xn = _layer_norm(ALPHA * x + f, g_ref[...], b_ref[...])
    gate = jax.nn.sigmoid(_bdot(xn, wgate_ref[...]))
    return xn + _bdot(p, wproj_ref[...]) * gate


def _ffn_kernel(*refs, tm, grouped, tiles_per_seq):
    if grouped:
        (x_ref, hdr_ref, p_ref, wup_ref, cw_ref, wdn_ref, g_ref, b_ref, wproj_ref, wgate_ref,
         y_ref, tail_ref, h_s) = refs
    else:
        (x_ref, p_ref, wup_ref, cw_ref, wdn_ref, g_ref, b_ref, wproj_ref, wgate_ref,
         y_ref, tail_ref, h_s, ext_s, carry_s) = refs

        @pl.when(pl.program_id(0) % tiles_per_seq == 0)
        def _():
            carry_s[...] = jnp.zeros_like(carry_s)

    x = x_ref[...]
    xb = x.astype(BF16)

    def conv(u, c0):
        cols = slice(c0, c0 + FFN_COLS)
        if grouped:
            return _conv_grouped(u, hdr_ref[:, cols], cols, FFN_CONV, cw_ref, tail_ref, SUBLANES)
        return _conv_carry(u, cols, FFN_CONV, cw_ref, ext_s, carry_s, tail_ref)

    for c in range(D_FF // FFN_COLS):
        c0 = c * FFN_COLS
        val = conv(jnp.dot(xb, wup_ref[:, c0:c0 + FFN_COLS], preferred_element_type=F32), c0)
        gat = conv(jnp.dot(xb, wup_ref[:, D_FF + c0:D_FF + c0 + FFN_COLS], preferred_element_type=F32),
                   D_FF + c0)
        h_s[:, c0:c0 + FFN_COLS] = (_silu(gat) * val).astype(BF16)

    f = jnp.dot(h_s[...], wdn_ref[...], preferred_element_type=F32)
    y_ref[...] = _residual_ln_ple(x, f, p_ref[...], g_ref, b_ref, wproj_ref, wgate_ref)


def _const_spec(shape):
    zeros = (0,) * len(shape)
    return pl.BlockSpec(shape, lambda i: zeros, pipeline_mode=pl.Buffered(1))


def _ffn_call(x, hdr, p, wup, cw, wdn, g, b, wproj, wgate, *, tm, seq_len):
    rows = x.shape[0]
    grouped = hdr is not None
    row_spec = lambda w: pl.BlockSpec((tm, w), lambda i: (i, 0))
    weights = [wup, cw, wdn, g, b, wproj, wgate]
    in_specs = [row_spec(D_MODEL)] + ([row_spec(2 * D_FF)] if grouped else []) + [row_spec(D_PLE)]
    in_specs += [_const_spec(w.shape) for w in weights]
    args = [x] + ([hdr] if grouped else []) + [p] + weights
    scratch = [pltpu.VMEM((tm, D_FF), BF16)]
    if grouped:
        tail_rows = rows
        tail_spec = row_spec(2 * D_FF)
        tiles_per_seq = 1
    else:
        tiles_per_seq = seq_len // tm
        tail_rows = rows // seq_len * SUBLANES
        tail_spec = pl.BlockSpec((SUBLANES, 2 * D_FF), lambda i: (i // tiles_per_seq, 0))
        scratch += [pltpu.VMEM((HDR_ROWS + tm, FFN_COLS), F32), pltpu.VMEM((HDR_ROWS, 2 * D_FF), F32)]
    return pl.pallas_call(
        functools.partial(_ffn_kernel, tm=tm, grouped=grouped, tiles_per_seq=tiles_per_seq),
        grid=(rows // tm,),
        in_specs=in_specs,
        out_specs=[row_spec(D_MODEL), tail_spec],
        out_shape=[jax.ShapeDtypeStruct((rows, D_MODEL), F32),
                   jax.ShapeDtypeStruct((tail_rows, 2 * D_FF), F32)],
        scratch_shapes=scratch,
        compiler_params=pltpu.CompilerParams(dimension_semantics=("arbitrary",),
                                             vmem_limit_bytes=VMEM_LIMIT_BYTES),
        name="ffn_grouped" if grouped else "ffn_seq",
    )(*args)


def _sc_kernel(*refs, tm, grouped, tiles_per_seq):
    if grouped:
        x_ref, hdr_ref, win_ref, cw_ref, wout_ref, g_ref, b_ref, y_ref, tail_ref, h_s = refs
    else:
        x_ref, win_ref, cw_ref, wout_ref, g_ref, b_ref, y_ref, tail_ref, h_s, ext_s, carry_s = refs

        @pl.when(pl.program_id(0) % tiles_per_seq == 0)
        def _():
            carry_s[...] = jnp.zeros_like(carry_s)

    x = x_ref[...]
    xb = x.astype(BF16)
    for c in range(D_MODEL // MIX_COLS):
        c0 = c * MIX_COLS
        cols = slice(c0, c0 + MIX_COLS)
        bb = jnp.dot(xb, win_ref[:, c0:c0 + MIX_COLS], preferred_element_type=F32)
        cc = jnp.dot(xb, win_ref[:, D_MODEL + c0:D_MODEL + c0 + MIX_COLS], preferred_element_type=F32)
        hh = jnp.dot(xb, win_ref[:, 2 * D_MODEL + c0:2 * D_MODEL + c0 + MIX_COLS], preferred_element_type=F32)
        u = cc * hh
        if grouped:
            uc = _conv_grouped(u, hdr_ref[:, cols], cols, SC_CONV, cw_ref, tail_ref, SUBLANES)
        else:
            uc = _conv_carry(u, cols, SC_CONV, cw_ref, ext_s, carry_s, tail_ref)
        h_s[:, cols] = (bb * uc).astype(BF16)
    mix = jnp.dot(h_s[...], wout_ref[...], preferred_element_type=F32)
    y_ref[...] = _layer_norm(ALPHA * x + mix, g_ref[...], b_ref[...])


def _sc_call(x, hdr, win, cw, wout, g, b, *, tm, seq_len):
    rows = x.shape[0]
    grouped = hdr is not None
    row_spec = lambda w: pl.BlockSpec((tm, w), lambda i: (i, 0))
    weights = [win, cw, wout, g, b]
    in_specs = [row_spec(D_MODEL)] + ([row_spec(D_MODEL)] if grouped else [])
    in_specs += [_const_spec(w.shape) for w in weights]
    args = [x] + ([hdr] if grouped else []) + weights
    scratch = [pltpu.VMEM((tm, D_MODEL), BF16)]
    if grouped:
        tail_rows = rows
        tail_spec = row_spec(D_MODEL)
        tiles_per_seq = 1
    else:
        tiles_per_seq = seq_len // tm
        tail_rows = rows // seq_len * SUBLANES
        tail_spec = pl.BlockSpec((SUBLANES, D_MODEL), lambda i: (i // tiles_per_seq, 0))
        scratch += [pltpu.VMEM((HDR_ROWS + tm, MIX_COLS), F32), pltpu.VMEM((HDR_ROWS, D_MODEL), F32)]
    return pl.pallas_call(
        functools.partial(_sc_kernel, tm=tm, grouped=grouped, tiles_per_seq=tiles_per_seq),
        grid=(rows // tm,),
        in_specs=in_specs,
        out_specs=[row_spec(D_MODEL), tail_spec],
        out_shape=[jax.ShapeDtypeStruct((rows, D_MODEL), F32),
                   jax.ShapeDtypeStruct((tail_rows, D_MODEL), F32)],
        scratch_shapes=scratch,
        compiler_params=pltpu.CompilerParams(dimension_semantics=("arbitrary",),
                                             vmem_limit_bytes=VMEM_LIMIT_BYTES),
        name="sc_grouped" if grouped else "sc_seq",
    )(*args)


def _dn_constants(sub, chunk):
    r = jnp.arange(sub)
    same = (r[:, None] // chunk) == (r[None, :] // chunk)
    causal = same & (r[:, None] >= r[None, :])
    strict = same & (r[:, None] > r[None, :])
    lane = jnp.arange(LANES)
    col = jnp.arange(N_HEADS * D_HEAD)
    head_of_col = col // D_HEAD
    return dict(
        ltri=causal.astype(BF16),
        utri=causal.T.astype(BF16),
        negmask=jnp.where(causal, 0.0, -jnp.inf).astype(F32),
        strict=strict.astype(F32),
        eye=jnp.eye(sub, dtype=F32),
        seg=(head_of_col[:, None] == lane[None, :]).astype(BF16),
        exp_b=(lane[:, None] == head_of_col[None, :]).astype(BF16),
        exp_g=(lane[:, None] == head_of_col[None, :] + N_HEADS).astype(BF16),
    )


def _dn_kernel(*refs, tm, sub, chunk, grouped, tiles_per_seq, sdt, aliased):
    n_in = 22 if grouped else 20
    ins, rest = refs[:n_in], refs[n_in:]
    if aliased:
        rest = rest[1:]
    if grouped:
        x_ref, hdr_ref, sin_ref = ins[:3]
        ins = ins[3:]
    else:
        x_ref = ins[0]
        ins = ins[1:]
    (win_ref, wbat_ref, cw_ref, alog_c_ref, dtb_c_ref, alog_r_ref, dtb_r_ref,
     ng_ref, wout_ref, g_ref, b_ref,
     ltri_ref, utri_ref, negmask_ref, strict_ref, eye_ref, seg_ref, expb_ref, expg_ref) = ins
    y_ref, tail_ref, sout_ref = rest[:3]
    scr = rest[3:]
    (qkv_s, z_s, ba_s, bat_s, o_s, k_s, kb_s, qs_s, qg_s, vb_s, kbg_s, gcb_s, gcr_s,
     att_s, u_s, w_s, vn_s) = scr[:17]
    if not grouped:
        ext_s, carry_s = scr[17:]

        @pl.when(pl.program_id(0) % tiles_per_seq == 0)
        def _():
            carry_s[...] = jnp.zeros_like(carry_s)
            sout_ref[...] = jnp.zeros_like(sout_ref)

    n_chunks = sub // chunk
    n_levels = int(math.log2(chunk)) - 1

    x = x_ref[...]
    xb = x.astype(BF16)
    hd = N_HEADS * D_HEAD
    proj = jnp.dot(xb, win_ref[...], preferred_element_type=F32)
    for c in range(QKV_DIM // MIX_COLS):
        c0 = c * MIX_COLS
        cols = slice(c0, c0 + MIX_COLS)
        u = proj[:, cols]
        if grouped:
            uc = _conv_grouped(u, hdr_ref[:, cols], cols, DN_CONV, cw_ref, tail_ref, SUBLANES)
        else:
            uc = _conv_carry(u, cols, DN_CONV, cw_ref, ext_s, carry_s, tail_ref)
        qkv_s[:, cols] = _silu(uc)
    z_s[...] = proj[:, QKV_DIM:QKV_DIM + hd]
    ba_s[...] = proj[:, QKV_DIM + hd:QKV_DIM + hd + LANES]
    bat_s[...] = lax.dot_general(wbat_ref[...], xb, (((1,), (1,)), ((), ())),
                                 preferred_element_type=F32)

    seg = seg_ref[...]
    expb = expb_ref[...]
    expg = expg_ref[...]
    hd = N_HEADS * D_HEAD

    def sub_tile(si, carry):
        rb = pl.multiple_of(si * sub, sub)
        rows = pl.ds(rb, sub)
        q = qkv_s[rows, 0:hd]
        k = qkv_s[rows, hd:2 * hd]
        v = qkv_s[rows, 2 * hd:3 * hd]
        rq = lax.rsqrt(_dot2(q * q, seg) + L2_EPS)
        rk = lax.rsqrt(_dot2(k * k, seg) + L2_EPS)
        qn = q * _dot2(rq, expb)
        kn = k * _dot2(rk, expb)

        ba = ba_s[rows, :]
        beta_c = jax.nn.sigmoid(ba)
        g_c = -jnp.exp(alog_c_ref[...]) * _softplus(ba + dtb_c_ref[...])
        gc_c = _dot3_l(ltri_ref[...], g_c)
        beta_b = _dot2(beta_c, expb)
        gc_b = _dot3(gc_c, expg)
        eg_b = jnp.exp(gc_b)

        bat = bat_s[:, rows]
        g_r = -jnp.exp(alog_r_ref[:, 0:sub]) * _softplus(bat + dtb_r_ref[:, 0:sub])
        gcr_s[...] = _dot3(g_r, utri_ref[...])

        kbeta = kn * beta_b
        k_s[...] = kn.astype(sdt)
        kb_s[...] = kbeta.astype(sdt)
        qs = qn * (D_HEAD ** -0.5)
        qs_s[...] = qs.astype(sdt)
        qg_s[...] = (qs * eg_b).astype(sdt)
        vb_s[...] = (v * beta_b).astype(sdt)
        kbg_s[...] = (kbeta * eg_b).astype(sdt)
        gcb_s[...] = gc_b

        negmask = negmask_ref[...]
        strict = strict_ref[...]
        eye = eye_ref[...]
        heads = range(N_HEADS)
        hsl = [slice(h * D_HEAD, (h + 1) * D_HEAD) for h in heads]
        kk = [_bdot_nt(kb_s[:, hsl[h]], k_s[:, hsl[h]]) for h in heads]
        qk = [_bdot_nt(qs_s[:, hsl[h]], k_s[:, hsl[h]]) for h in heads]
        bm, pm = [], []
        for h in heads:
            gcb = gcb_s[:, hsl[h]]
            gcw = gcb[:, 0:sub] if sub <= D_HEAD else jnp.concatenate([gcb] * (sub // D_HEAD), axis=1)
            gcr = gcr_s[N_HEADS + h:N_HEADS + h + 1, :]
            dec = jnp.exp(gcw - gcr + negmask)
            att_s[h] = (qk[h] * dec).astype(sdt)
            b0 = -(kk[h] * dec * strict)
            bm.append(b0)
            pm.append(eye + b0)
        bm = [_bdot(b, b) for b in bm]
        for _ in range(1, n_levels):
            r = [_bdot(bm[h], jnp.concatenate([bm[h], pm[h]], axis=1)) for h in heads]
            bm = [r[h][:, 0:sub] for h in heads]
            pm = [pm[h] + r[h][:, sub:2 * sub] for h in heads]
        pm = [pm[h] + _bdot(bm[h], pm[h]) for h in heads]
        sol = [_bdot(pm[h], jnp.concatenate([vb_s[:, hsl[h]], kbg_s[:, hsl[h]]], axis=1)) for h in heads]
        for h in heads:
            u_s[:, hsl[h]] = sol[h][:, 0:D_HEAD]
            w_s[:, hsl[h]] = sol[h][:, D_HEAD:2 * D_HEAD]
        vn_s[...] = jnp.zeros_like(vn_s)

        def chunk_step(c, carry_c):
            r0 = pl.multiple_of(c * chunk, chunk)
            cr = pl.ds(r0, chunk)
            if grouped:
                s_old = [sin_ref[si * n_chunks + c, h] for h in heads]
            else:
                s_old = [sout_ref[h] for h in heads]
            ws = [_bdot(jnp.concatenate([w_s[cr, hsl[h]], qg_s[cr, hsl[h]].astype(F32)], axis=0), s_old[h])
                  for h in heads]
            v_new = [u_s[cr, hsl[h]] - ws[h][0:chunk] for h in heads]
            for h in heads:
                vn_s[cr, hsl[h]] = v_new[h].astype(sdt)
            for h in heads:
                o_s[pl.ds(rb + r0, chunk), hsl[h]] = (ws[h][chunk:2 * chunk]
                                                      + _bdot(att_s[h, cr, :], vn_s[:, hsl[h]]))
            for h in heads:
                gc_rows = gcb_s[cr, hsl[h]]
                g_last = gc_rows[chunk - 1:chunk, :]
                kd = k_s[cr, hsl[h]].astype(F32) * jnp.exp(g_last - gc_rows)
                s_new = s_old[h] * jnp.exp(g_last) + _bdot_tn(kd, v_new[h])
                if grouped:
                    sout_ref[si * n_chunks + c, h] = s_new
                else:
                    sout_ref[h] = s_new
            return carry_c

        lax.fori_loop(0, n_chunks, chunk_step, 0)
        return carry

    lax.fori_loop(0, tm // sub, sub_tile, 0)

    o = o_s[...]
    ms = _dot2(o * o, seg) * (1.0 / D_HEAD)
    o = o * _dot2(lax.rsqrt(ms + RMS_EPS), expb) * ng_ref[...] * _silu(z_s[...])
    mix = _bdot(o, wout_ref[...])
    y_ref[...] = _layer_norm(ALPHA * x + mix, g_ref[...], b_ref[...])


def _dn_call(x, hdr, s_in, weights, consts, *, tm, sub, chunk, seq_len, layer=0, s_prev=None):
    rows = x.shape[0]
    grouped = hdr is not None
    aliases = {}
    sdt = BF16 if chunk % (2 * SUBLANES) == 0 else F32
    row_spec = lambda w: pl.BlockSpec((tm, w), lambda i: (i, 0))
    const_list = [consts[k] for k in ("ltri", "utri", "negmask", "strict", "eye", "seg", "exp_b", "exp_g")]
    in_specs = [row_spec(D_MODEL)]
    args = [x]
    hd = N_HEADS * D_HEAD
    if grouped:
        seqs_per_tile = tm // chunk
        s_spec = pl.BlockSpec((None, seqs_per_tile, N_HEADS, D_HEAD, D_HEAD), lambda i: (layer, i, 0, 0, 0))
        in_specs += [row_spec(QKV_DIM), s_spec]
        args += [hdr, s_in]
        tail_rows, tail_spec = rows, row_spec(QKV_DIM)
        s_out_shape = s_in.shape
        s_out_spec = s_spec
        tiles_per_seq = 1
    else:
        tiles_per_seq = seq_len // tm
        n_seq = rows // seq_len
        tail_rows = n_seq * SUBLANES
        tail_spec = pl.BlockSpec((SUBLANES, QKV_DIM), lambda i: (i // tiles_per_seq, 0))
        s_out_shape = (n_seq, N_HEADS, D_HEAD, D_HEAD)
        s_out_spec = pl.BlockSpec((None, N_HEADS, D_HEAD, D_HEAD), lambda i: (i // tiles_per_seq, 0, 0, 0))
    in_specs += [_const_spec(w.shape) for w in list(weights) + const_list]
    args += list(weights) + const_list
    if s_prev is not None:
        in_specs.append(pl.BlockSpec(memory_space=pl.ANY))
        args.append(s_prev)
        aliases = {len(args) - 1: 2}
    scratch = [
        pltpu.VMEM((tm, QKV_DIM), F32),
        pltpu.VMEM((tm, hd), F32),
        pltpu.VMEM((tm, LANES), F32),
        pltpu.VMEM((2 * N_HEADS, tm), F32),
        pltpu.VMEM((tm, hd), F32),
        pltpu.VMEM((sub, hd), sdt),
        pltpu.VMEM((sub, hd), sdt),
        pltpu.VMEM((sub, hd), sdt),
        pltpu.VMEM((sub, hd), sdt),
        pltpu.VMEM((sub, hd), sdt),
        pltpu.VMEM((sub, hd), sdt),
        pltpu.VMEM((sub, hd), F32),
        pltpu.VMEM((2 * N_HEADS, sub), F32),
        pltpu.VMEM((N_HEADS, sub, sub), sdt),
        pltpu.VMEM((sub, hd), F32),
        pltpu.VMEM((sub, hd), F32),
        pltpu.VMEM((sub, hd), sdt),
    ]
    if not grouped:
        scratch += [pltpu.VMEM((HDR_ROWS + tm, MIX_COLS), F32), pltpu.VMEM((HDR_ROWS, QKV_DIM), F32)]
    return pl.pallas_call(
        functools.partial(_dn_kernel, tm=tm, sub=sub, chunk=chunk, grouped=grouped,
                          tiles_per_seq=tiles_per_seq, sdt=sdt, aliased=s_prev is not None),
        grid=(rows // tm,),
        input_output_aliases=aliases,
        in_specs=in_specs,
        out_specs=[row_spec(D_MODEL), tail_spec, s_out_spec],
        out_shape=[jax.ShapeDtypeStruct((rows, D_MODEL), F32),
                   jax.ShapeDtypeStruct((tail_rows, QKV_DIM), F32),
                   jax.ShapeDtypeStruct(s_out_shape, F32)],
        scratch_shapes=scratch,
        compiler_params=pltpu.CompilerParams(dimension_semantics=("arbitrary",),
                                             vmem_limit_bytes=VMEM_LIMIT_BYTES),
        name="dn_grouped" if grouped else "dn_seq",
    )(*args)


def _pad_state(state, width):
    b, _, c = state.shape
    return jnp.pad(state, ((0, 0), (SUBLANES - (width - 1), 0), (0, 0))).reshape(b * SUBLANES, c)


def _tail_state(tail, width):
    c = tail.shape[-1]
    return tail.reshape(-1, SUBLANES, c)[:, SUBLANES - (width - 1):, :]


def _row(v):
    return v.reshape(1, -1).astype(F32)


def kernel(x_prompt, x_sample, p_prompt, p_sample, state_dn_conv, state_dn_S, state_sc_conv, state_ffn_conv, dn_w_in, dn_conv_w, dn_a_log, dn_dt_bias, dn_norm_g, dn_w_out, sc_w_in, sc_conv_w, sc_w_out, ffn_w_up, ffn_conv_w, ffn_w_down, ln_mix_g, ln_mix_b, ln_ffn_g, ln_ffn_b, ple_w_proj, ple_w_gate):
    bp, lp, _ = x_prompt.shape
    bs, ls, _ = x_sample.shape
    assert ls == SUBLANES and lp % 512 == 0
    hd = N_HEADS * D_HEAD
    xp = x_prompt.reshape(bp * lp, D_MODEL)
    xs = x_sample.reshape(bs * ls, D_MODEL)
    consts_p = _dn_constants(128, CHUNK)
    consts_s = _dn_constants(64, min(CHUNK, ls))

    outs_p = {"dn_conv": [], "dn_s": [], "sc": [], "ffn": []}
    outs_s = {"dn_conv": [], "sc": [], "ffn": []}
    s_new_sample = None
    for i in range(DEPTH):
        j = i // 2
        if i % 2 == 0:
            w_in = dn_w_in[j]
            w_ba = w_in[:, QKV_DIM + hd:]
            lane_pad = LANES - 2 * N_HEADS
            zeros8 = jnp.zeros((N_HEADS,), F32)
            alog16 = jnp.concatenate([zeros8, dn_a_log[j].astype(F32)])
            dtb16 = jnp.concatenate([zeros8, dn_dt_bias[j].astype(F32)])
            weights = [
                jnp.pad(w_in, ((0, 0), (0, lane_pad))).astype(BF16),
                w_ba.T.astype(BF16),
                dn_conv_w[j].astype(F32),
                jnp.pad(alog16, (0, lane_pad)).reshape(1, LANES),
                jnp.pad(dtb16, (0, lane_pad)).reshape(1, LANES),
                jnp.broadcast_to(alog16[:, None], (2 * N_HEADS, 256)),
                jnp.broadcast_to(dtb16[:, None], (2 * N_HEADS, 256)),
                _row(jnp.tile(dn_norm_g[j], N_HEADS)),
                dn_w_out[j].astype(BF16),
                _row(ln_mix_g[i]), _row(ln_mix_b[i]),
            ]
            xp, tail, s_out = _dn_call(xp, None, None, weights, consts_p, tm=512, sub=128, chunk=CHUNK, seq_len=lp)
            outs_p["dn_conv"].append(_tail_state(tail, DN_CONV))
            outs_p["dn_s"].append(s_out)
            xs, tail, s_new_sample = _dn_call(xs, _pad_state(state_dn_conv[j], DN_CONV), state_dn_S, weights,
                                              consts_s, tm=64, sub=64, chunk=ls, seq_len=ls,
                                              layer=j, s_prev=s_new_sample)
            outs_s["dn_conv"].append(_tail_state(tail, DN_CONV))
        else:
            weights = [sc_w_in[j].astype(BF16), sc_conv_w[j].astype(F32), sc_w_out[j].astype(BF16),
                       _row(ln_mix_g[i]), _row(ln_mix_b[i])]
            xp, tail = _sc_call(xp, None, *weights, tm=512, seq_len=lp)
            outs_p["sc"].append(_tail_state(tail, SC_CONV))
            xs, tail = _sc_call(xs, _pad_state(state_sc_conv[j], SC_CONV), *weights, tm=512, seq_len=ls)
            outs_s["sc"].append(_tail_state(tail, SC_CONV))
        weights = [ffn_w_up[i].astype(BF16), ffn_conv_w[i].astype(F32), ffn_w_down[i].astype(BF16),
                   _row(ln_ffn_g[i]), _row(ln_ffn_b[i]), ple_w_proj[i].astype(BF16), ple_w_gate[i].astype(BF16)]
        xp, tail = _ffn_call(xp, None, p_prompt[i].reshape(bp * lp, D_PLE), *weights, tm=512, seq_len=lp)
        outs_p["ffn"].append(_tail_state(tail, FFN_CONV))
        xs, tail = _ffn_call(xs, _pad_state(state_ffn_conv[i], FFN_CONV), p_sample[i].reshape(bs * ls, D_PLE),
                             *weights, tm=128, seq_len=ls)
        outs_s["ffn"].append(_tail_state(tail, FFN_CONV))

    return (xp.reshape(bp, lp, D_MODEL), xs.reshape(bs, ls, D_MODEL),
            jnp.stack(outs_p["dn_conv"]), jnp.stack(outs_p["dn_s"]), jnp.stack(outs_p["sc"]), jnp.stack(outs_p["ffn"]),
            jnp.stack(outs_s["dn_conv"]), s_new_sample, jnp.stack(outs_s["sc"]), jnp.stack(outs_s["ffn"]))
```

```python
import functools
import math

import jax
import jax.numpy as jnp
from jax import lax
from jax.experimental import pallas as pl
from jax.experimental.pallas import tpu as pltpu

F32 = jnp.float32
BF16 = jnp.bfloat16

D_MODEL = 1024
DEPTH = 4
N_HEADS = 8
D_HEAD = 128
QKV_DIM = 3 * N_HEADS * D_HEAD
DN_CONV = 4
CHUNK = 64
SC_CONV = 3
D_FF = 2816
FFN_CONV = 3
D_PLE = 256
ALPHA = (2.0 * DEPTH) ** 0.25
LN_EPS = 1e-5
RMS_EPS = 1e-6
L2_EPS = 1e-6

LANES = 128
SUBLANES = 8
VMEM_LIMIT_BYTES = 56 * 1024 * 1024
HDR_ROWS = 16
FFN_COLS = 256
MIX_COLS = 512


def _bdot(a, b):
    return jnp.dot(a.astype(BF16), b.astype(BF16), preferred_element_type=F32)


def _bdot_nt(a, b):
    return lax.dot_general(a.astype(BF16), b.astype(BF16), (((1,), (1,)), ((), ())),
                           preferred_element_type=F32)


def _bdot_tn(a, b):
    return lax.dot_general(a.astype(BF16), b.astype(BF16), (((0,), (0,)), ((), ())),
                           preferred_element_type=F32)


def _split3(x):
    hi = x.astype(BF16)
    r = x - hi.astype(F32)
    mid = r.astype(BF16)
    lo = (r - mid.astype(F32)).astype(BF16)
    return hi, mid, lo


def _dot3(x, e3):
    return jnp.dot(jnp.concatenate(_split3(x), axis=1), e3, preferred_element_type=F32)


def _dot2(x, e2):
    hi = x.astype(BF16)
    lo = (x - hi.astype(F32)).astype(BF16)
    return jnp.dot(jnp.concatenate([hi, lo], axis=1), e2, preferred_element_type=F32)


def _dot3_l(e, x):
    n = x.shape[1]
    r = jnp.dot(e, jnp.concatenate(_split3(x), axis=1), preferred_element_type=F32)
    return r[:, 0:n] + r[:, n:2 * n] + r[:, 2 * n:3 * n]


def _layer_norm(x, g, b):
    mu = jnp.mean(x, axis=-1, keepdims=True)
    xc = x - mu
    var = jnp.mean(xc * xc, axis=-1, keepdims=True)
    return xc * lax.rsqrt(var + LN_EPS) * g + b


def _silu(x):
    return x * jax.nn.sigmoid(x)


def _softplus(x):
    return jnp.maximum(x, 0.0) + jnp.log1p(jnp.exp(-jnp.abs(x)))


def _conv_carry(u, cols, width, cw_ref, carry_ref, tail_ref):
    tm = u.shape[0]
    head = u[0:HDR_ROWS, :]
    prev = carry_ref[:, cols]
    t = lax.broadcasted_iota(jnp.int32, head.shape, 0)
    y = None
    yh = None
    for j in range(width):
        d = width - 1 - j
        w = cw_ref[j:j + 1, cols]
        term = (u if d == 0 else pltpu.roll(u, d, 0)) * w
        y = term if y is None else y + term
        src = head if d == 0 else jnp.where(t >= d, pltpu.roll(head, d, 0), pltpu.roll(prev, d, 0))
        term = src * w
        yh = term if yh is None else yh + term
    carry_ref[:, cols] = u[tm - HDR_ROWS:tm, :]
    tail_ref[:, cols] = u[tm - SUBLANES:tm, :]
    return jnp.concatenate([yh, y[HDR_ROWS:tm, :]], axis=0)


def _conv_grouped(u, hdr, cols, width, cw_ref, tail_ref, period):
    n = u.shape[0]
    assert period & (period - 1) == 0
    t = lax.broadcasted_iota(jnp.int32, u.shape, 0) & (period - 1)
    y = None
    for j in range(width):
        d = width - 1 - j
        if d == 0:
            src = u
        else:
            src = jnp.where(t >= d, pltpu.roll(u, d, 0), pltpu.roll(hdr, (n - period + d) % n, 0))
        term = src * cw_ref[j:j + 1, cols]
        y = term if y is None else y + term
    tail_ref[:, cols] = u
    return y


def _residual_ln_ple(x, f, p, g_ref, b_ref, wproj_ref, wgate_ref):
    xn = _layer_norm(ALPHA * x + f, g_ref[...], b_ref[...])
    gate = jax.nn.sigmoid(_bdot(xn, wgate_ref[...]))
    return xn + _bdot(p, wproj_ref[...]) * gate


def _ffn_kernel(*refs, tm, grouped, tiles_per_seq):
    if grouped:
        (x_ref, hdr_ref, p_ref, wup_ref, cw_ref, wdn_ref, g_ref, b_ref, wproj_ref, wgate_ref,
         y_ref, tail_ref, h_s) = refs
    else:
        (x_ref, p_ref, wup_ref, cw_ref, wdn_ref, g_ref, b_ref, wproj_ref, wgate_ref,
         y_ref, tail_ref, h_s, carry_s) = refs

        @pl.when(pl.program_id(0) % tiles_per_seq == 0)
        def _():
            carry_s[...] = jnp.zeros_like(carry_s)

    x = x_ref[...]
    xb = x.astype(BF16)

    def conv(u, c0):
        cols = slice(c0, c0 + FFN_COLS)
        if grouped:
            return _conv_grouped(u, hdr_ref[:, cols], cols, FFN_CONV, cw_ref, tail_ref, SUBLANES)
        return _conv_carry(u, cols, FFN_CONV, cw_ref, carry_s, tail_ref)

    for c in range(D_FF // FFN_COLS):
        c0 = c * FFN_COLS
        val = conv(jnp.dot(xb, wup_ref[:, c0:c0 + FFN_COLS], preferred_element_type=F32), c0)
        gat = conv(jnp.dot(xb, wup_ref[:, D_FF + c0:D_FF + c0 + FFN_COLS], preferred_element_type=F32),
                   D_FF + c0)
        h_s[:, c0:c0 + FFN_COLS] = (_silu(gat) * val).astype(BF16)

    f = jnp.dot(h_s[...], wdn_ref[...], preferred_element_type=F32)
    y_ref[...] = _residual_ln_ple(x, f, p_ref[...], g_ref, b_ref, wproj_ref, wgate_ref)


def _const_spec(shape):
    zeros = (0,) * len(shape)
    return pl.BlockSpec(shape, lambda i: zeros, pipeline_mode=pl.Buffered(1))


def _ffn_call(x, hdr, p, wup, cw, wdn, g, b, wproj, wgate, *, tm, seq_len, layer):
    rows = x.shape[0]
    grouped = hdr is not None
    row_spec = lambda w: pl.BlockSpec((tm, w), lambda i: (i, 0))
    weights = [wup, cw, wdn, g, b, wproj, wgate]
    p_spec = pl.BlockSpec((None, tm, D_PLE), lambda i: (layer, i, 0))
    in_specs = [row_spec(D_MODEL)] + ([row_spec(2 * D_FF)] if grouped else []) + [p_spec]
    in_specs += [_const_spec(w.shape) for w in weights]
    args = [x] + ([hdr] if grouped else []) + [p] + weights
    scratch = [pltpu.VMEM((tm, D_FF), BF16)]
    if grouped:
        tail_rows = rows
        tail_spec = row_spec(2 * D_FF)
        tiles_per_seq = 1
    else:
        tiles_per_seq = seq_len // tm
        tail_rows = rows // seq_len * SUBLANES
        tail_spec = pl.BlockSpec((SUBLANES, 2 * D_FF), lambda i: (i // tiles_per_seq, 0))
        scratch += [pltpu.VMEM((HDR_ROWS, 2 * D_FF), F32)]
    return pl.pallas_call(
        functools.partial(_ffn_kernel, tm=tm, grouped=grouped, tiles_per_seq=tiles_per_seq),
        grid=(rows // tm,),
        in_specs=in_specs,
        out_specs=[row_spec(D_MODEL), tail_spec],
        out_shape=[jax.ShapeDtypeStruct((rows, D_MODEL), F32),
                   jax.ShapeDtypeStruct((tail_rows, 2 * D_FF), F32)],
        scratch_shapes=scratch,
        compiler_params=pltpu.CompilerParams(dimension_semantics=("arbitrary",),
                                             vmem_limit_bytes=VMEM_LIMIT_BYTES),
        name="ffn_grouped" if grouped else "ffn_seq",
    )(*args)


def _sc_kernel(*refs, tm, grouped, tiles_per_seq):
    if grouped:
        x_ref, hdr_ref, win_ref, cw_ref, wout_ref, g_ref, b_ref, y_ref, tail_ref, h_s = refs
    else:
        x_ref, win_ref, cw_ref, wout_ref, g_ref, b_ref, y_ref, tail_ref, h_s, carry_s = refs

        @pl.when(pl.program_id(0) % tiles_per_seq == 0)
        def _():
            carry_s[...] = jnp.zeros_like(carry_s)

    x = x_ref[...]
    xb = x.astype(BF16)
    for c in range(D_MODEL // MIX_COLS):
        c0 = c * MIX_COLS
        cols = slice(c0, c0 + MIX_COLS)
        bb = jnp.dot(xb, win_ref[:, c0:c0 + MIX_COLS], preferred_element_type=F32)
        cc = jnp.dot(xb, win_ref[:, D_MODEL + c0:D_MODEL + c0 + MIX_COLS], preferred_element_type=F32)
        hh = jnp.dot(xb, win_ref[:, 2 * D_MODEL + c0:2 * D_MODEL + c0 + MIX_COLS], preferred_element_type=F32)
        u = cc * hh
        if grouped:
            uc = _conv_grouped(u, hdr_ref[:, cols], cols, SC_CONV, cw_ref, tail_ref, SUBLANES)
        else:
            uc = _conv_carry(u, cols, SC_CONV, cw_ref, carry_s, tail_ref)
        h_s[:, cols] = (bb * uc).astype(BF16)
    mix = jnp.dot(h_s[...], wout_ref[...], preferred_element_type=F32)
    y_ref[...] = _layer_norm(ALPHA * x + mix, g_ref[...], b_ref[...])


def _sc_call(x, hdr, win, cw, wout, g, b, *, tm, seq_len):
    rows = x.shape[0]
    grouped = hdr is not None
    row_spec = lambda w: pl.BlockSpec((tm, w), lambda i: (i, 0))
    weights = [win, cw, wout, g, b]
    in_specs = [row_spec(D_MODEL)] + ([row_spec(D_MODEL)] if grouped else [])
    in_specs += [_const_spec(w.shape) for w in weights]
    args = [x] + ([hdr] if grouped else []) + weights
    scratch = [pltpu.VMEM((tm, D_MODEL), BF16)]
    if grouped:
        tail_rows = rows
        tail_spec = row_spec(D_MODEL)
        tiles_per_seq = 1
    else:
        tiles_per_seq = seq_len // tm
        tail_rows = rows // seq_len * SUBLANES
        tail_spec = pl.BlockSpec((SUBLANES, D_MODEL), lambda i: (i // tiles_per_seq, 0))
        scratch += [pltpu.VMEM((HDR_ROWS, D_MODEL), F32)]
    return pl.pallas_call(
        functools.partial(_sc_kernel, tm=tm, grouped=grouped, tiles_per_seq=tiles_per_seq),
        grid=(rows // tm,),
        in_specs=in_specs,
        out_specs=[row_spec(D_MODEL), tail_spec],
        out_shape=[jax.ShapeDtypeStruct((rows, D_MODEL), F32),
                   jax.ShapeDtypeStruct((tail_rows, D_MODEL), F32)],
        scratch_shapes=scratch,
        compiler_params=pltpu.CompilerParams(dimension_semantics=("arbitrary",),
                                             vmem_limit_bytes=VMEM_LIMIT_BYTES),
        name="sc_grouped" if grouped else "sc_seq",
    )(*args)


def _dn_constants(sub, chunk):
    r = jnp.arange(sub)
    same = (r[:, None] // chunk) == (r[None, :] // chunk)
    causal = same & (r[:, None] >= r[None, :])
    strict = same & (r[:, None] > r[None, :])
    lane = jnp.arange(LANES)
    col = jnp.arange(N_HEADS * D_HEAD)
    head_of_col = col // D_HEAD
    seg = (head_of_col[:, None] == lane[None, :]).astype(BF16)
    exp_b = (lane[:, None] == head_of_col[None, :]).astype(BF16)
    exp_g = (lane[:, None] == head_of_col[None, :] + N_HEADS).astype(BF16)
    utri = causal.T.astype(BF16)
    return dict(
        ltri=causal.astype(BF16),
        utri=jnp.concatenate([utri] * 3, axis=0),
        negmask=jnp.where(causal, 0.0, -jnp.inf).astype(F32),
        strict=strict.astype(F32),
        eye=jnp.eye(sub, dtype=F32),
        seg=jnp.concatenate([seg] * 2, axis=0),
        exp_b=jnp.concatenate([exp_b] * 2, axis=0),
        exp_g=jnp.concatenate([exp_g] * 3, axis=0),
    )


def _dn_kernel(*refs, tm, sub, chunk, grouped, tiles_per_seq, sdt, aliased):
    n_in = 22 if grouped else 20
    ins, rest = refs[:n_in], refs[n_in:]
    if aliased:
        rest = rest[1:]
    if grouped:
        x_ref, hdr_ref, sin_ref = ins[:3]
        ins = ins[3:]
    else:
        x_ref = ins[0]
        ins = ins[1:]
    (win_ref, wbat_ref, cw_ref, alog_c_ref, dtb_c_ref, alog_r_ref, dtb_r_ref,
     ng_ref, wout_ref, g_ref, b_ref,
     ltri_ref, utri_ref, negmask_ref, strict_ref, eye_ref, seg_ref, expb_ref, expg_ref) = ins
    y_ref, tail_ref, sout_ref = rest[:3]
    scr = rest[3:]
    (qkv_s, z_s, ba_s, bat_s, o_s, k_s, kb_s, qs_s, qg_s, vb_s, kbg_s, gcb_s, gcr_s,
     att_s, u_s, w_s, vn_s) = scr[:17]
    if not grouped:
        (carry_s,) = scr[17:]

        @pl.when(pl.program_id(0) % tiles_per_seq == 0)
        def _():
            carry_s[...] = jnp.zeros_like(carry_s)
            sout_ref[...] = jnp.zeros_like(sout_ref)

    n_chunks = sub // chunk
    n_levels = int(math.log2(chunk)) - 1

    x = x_ref[...]
    xb = x.astype(BF16)
    hd = N_HEADS * D_HEAD
    proj = jnp.dot(xb, win_ref[...], preferred_element_type=F32)
    for c in range(QKV_DIM // MIX_COLS):
        c0 = c * MIX_COLS
        cols = slice(c0, c0 + MIX_COLS)
        u = proj[:, cols]
        if grouped:
            uc = _conv_grouped(u, hdr_ref[:, cols], cols, DN_CONV, cw_ref, tail_ref, SUBLANES)
        else:
            uc = _conv_carry(u, cols, DN_CONV, cw_ref, carry_s, tail_ref)
        qkv_s[:, cols] = _silu(uc)
    z_s[...] = proj[:, QKV_DIM:QKV_DIM + hd]
    ba_s[...] = proj[:, QKV_DIM + hd:QKV_DIM + hd + LANES]
    bat_s[...] = lax.dot_general(wbat_ref[...], xb, (((1,), (1,)), ((), ())),
                                 preferred_element_type=F32)

    seg = seg_ref[...]
    expb = expb_ref[...]
    expg = expg_ref[...]
    hd = N_HEADS * D_HEAD

    def sub_tile(si, carry):
        rb = pl.multiple_of(si * sub, sub)
        rows = pl.ds(rb, sub)
        q = qkv_s[rows, 0:hd]
        k = qkv_s[rows, hd:2 * hd]
        v = qkv_s[rows, 2 * hd:3 * hd]
        rq = lax.rsqrt(_dot2(q * q, seg) + L2_EPS)
        rk = lax.rsqrt(_dot2(k * k, seg) + L2_EPS)
        qn = q * _dot2(rq, expb)
        kn = k * _dot2(rk, expb)

        ba = ba_s[rows, :]
        beta_c = jax.nn.sigmoid(ba)
        g_c = -jnp.exp(alog_c_ref[...]) * _softplus(ba + dtb_c_ref[...])
        gc_c = _dot3_l(ltri_ref[...], g_c)
        beta_b = _dot2(beta_c, expb)
        gc_b = _dot3(gc_c, expg)
        eg_b = jnp.exp(gc_b)

        bat = bat_s[:, rows]
        g_r = -jnp.exp(alog_r_ref[:, 0:sub]) * _softplus(bat + dtb_r_ref[:, 0:sub])
        gcr_s[...] = _dot3(g_r, utri_ref[...])

        kbeta = kn * beta_b
        k_s[...] = kn.astype(sdt)
        kb_s[...] = kbeta.astype(sdt)
        qs = qn * (D_HEAD ** -0.5)
        qs_s[...] = qs.astype(sdt)
        qg_s[...] = (qs * eg_b).astype(sdt)
        vb_s[...] = (v * beta_b).astype(sdt)
        kbg_s[...] = (kbeta * eg_b).astype(sdt)
        gcb_s[...] = gc_b

        negmask = negmask_ref[...]
        strict = strict_ref[...]
        eye = eye_ref[...]
        heads = range(N_HEADS)
        hsl = [slice(h * D_HEAD, (h + 1) * D_HEAD) for h in heads]
        kk = [_bdot_nt(kb_s[:, hsl[h]], k_s[:, hsl[h]]) for h in heads]
        qk = [_bdot_nt(qs_s[:, hsl[h]], k_s[:, hsl[h]]) for h in heads]
        bm, pm = [], []
        for h in heads:
            gcb = gcb_s[:, hsl[h]]
            gcw = gcb[:, 0:sub] if sub <= D_HEAD else jnp.concatenate([gcb] * (sub // D_HEAD), axis=1)
            gcr = gcr_s[N_HEADS + h:N_HEADS + h + 1, :]
            dec = jnp.exp(gcw - gcr + negmask)
            att_s[h] = (qk[h] * dec).astype(sdt)
            b0 = -(kk[h] * dec * strict)
            bm.append(b0)
            pm.append(eye + b0)
        bm = [_bdot(b, b) for b in bm]
        for _ in range(1, n_levels):
            r = [_bdot(bm[h], jnp.concatenate([bm[h], pm[h]], axis=1)) for h in heads]
            bm = [r[h][:, 0:sub] for h in heads]
            pm = [pm[h] + r[h][:, sub:2 * sub] for h in heads]
        pm = [pm[h] + _bdot(bm[h], pm[h]) for h in heads]
        sol = [_bdot(pm[h], jnp.concatenate([vb_s[:, hsl[h]], kbg_s[:, hsl[h]]], axis=1)) for h in heads]
        for h in heads:
            u_s[:, hsl[h]] = sol[h][:, 0:D_HEAD]
            w_s[:, hsl[h]] = sol[h][:, D_HEAD:2 * D_HEAD]
        vn_s[...] = jnp.zeros_like(vn_s)

        def chunk_step(c, carry_c):
            r0 = pl.multiple_of(c * chunk, chunk)
            cr = pl.ds(r0, chunk)
            if grouped:
                s_old = [sin_ref[si * n_chunks + c, h] for h in heads]
            else:
                s_old = [sout_ref[h] for h in heads]
            ws = [_bdot(jnp.concatenate([w_s[cr, hsl[h]], qg_s[cr, hsl[h]].astype(F32)], axis=0), s_old[h])
                  for h in heads]
            v_new = [u_s[cr, hsl[h]] - ws[h][0:chunk] for h in heads]
            for h in heads:
                vn_s[cr, hsl[h]] = v_new[h].astype(sdt)
            for h in heads:
                o_s[pl.ds(rb + r0, chunk), hsl[h]] = (ws[h][chunk:2 * chunk]
                                                      + _bdot(att_s[h, cr, :], vn_s[:, hsl[h]]))
            for h in heads:
                gc_rows = gcb_s[cr, hsl[h]]
                g_last = gc_rows[chunk - 1:chunk, :]
                kd = k_s[cr, hsl[h]].astype(F32) * jnp.exp(g_last - gc_rows)
                s_new = s_old[h] * jnp.exp(g_last) + _bdot_tn(kd, v_new[h])
                if grouped:
                    sout_ref[si * n_chunks + c, h] = s_new
                else:
                    sout_ref[h] = s_new
            return carry_c

        lax.fori_loop(0, n_chunks, chunk_step, 0)
        return carry

    lax.fori_loop(0, tm // sub, sub_tile, 0)

    o = o_s[...]
    ms = _dot2(o * o, seg) * (1.0 / D_HEAD)
    o = o * _dot2(lax.rsqrt(ms + RMS_EPS), expb) * ng_ref[...] * _silu(z_s[...])
    mix = _bdot(o, wout_ref[...])
    y_ref[...] = _layer_norm(ALPHA * x + mix, g_ref[...], b_ref[...])


def _dn_call(x, hdr, s_in, weights, consts, *, tm, sub, chunk, seq_len, layer=0, s_prev=None):
    rows = x.shape[0]
    grouped = hdr is not None
    aliases = {}
    sdt = BF16 if chunk % (2 * SUBLANES) == 0 else F32
    row_spec = lambda w: pl.BlockSpec((tm, w), lambda i: (i, 0))
    const_list = [consts[k] for k in ("ltri", "utri", "negmask", "strict", "eye", "seg", "exp_b", "exp_g")]
    in_specs = [row_spec(D_MODEL)]
    args = [x]
    hd = N_HEADS * D_HEAD
    if grouped:
        seqs_per_tile = tm // chunk
        s_spec = pl.BlockSpec((None, seqs_per_tile, N_HEADS, D_HEAD, D_HEAD), lambda i: (layer, i, 0, 0, 0))
        in_specs += [row_spec(QKV_DIM), s_spec]
        args += [hdr, s_in]
        tail_rows, tail_spec = rows, row_spec(QKV_DIM)
        s_out_shape = s_in.shape
        s_out_spec = s_spec
        tiles_per_seq = 1
    else:
        tiles_per_seq = seq_len // tm
        n_seq = rows // seq_len
        tail_rows = n_seq * SUBLANES
        tail_spec = pl.BlockSpec((SUBLANES, QKV_DIM), lambda i: (i // tiles_per_seq, 0))
        s_out_shape = (n_seq, N_HEADS, D_HEAD, D_HEAD)
        s_out_spec = pl.BlockSpec((None, N_HEADS, D_HEAD, D_HEAD), lambda i: (i // tiles_per_seq, 0, 0, 0))
    in_specs += [_const_spec(w.shape) for w in list(weights) + const_list]
    args += list(weights) + const_list
    if s_prev is not None:
        in_specs.append(pl.BlockSpec(memory_space=pl.ANY))
        args.append(s_prev)
        aliases = {len(args) - 1: 2}
    scratch = [
        pltpu.VMEM((tm, QKV_DIM), F32),
        pltpu.VMEM((tm, hd), F32),
        pltpu.VMEM((tm, LANES), F32),
        pltpu.VMEM((2 * N_HEADS, tm), F32),
        pltpu.VMEM((tm, hd), F32),
        pltpu.VMEM((sub, hd), sdt),
        pltpu.VMEM((sub, hd), sdt),
        pltpu.VMEM((sub, hd), sdt),
        pltpu.VMEM((sub, hd), sdt),
        pltpu.VMEM((sub, hd), sdt),
        pltpu.VMEM((sub, hd), sdt),
        pltpu.VMEM((sub, hd), F32),
        pltpu.VMEM((2 * N_HEADS, sub), F32),
        pltpu.VMEM((N_HEADS, sub, sub), sdt),
        pltpu.VMEM((sub, hd), F32),
        pltpu.VMEM((sub, hd), F32),
        pltpu.VMEM((sub, hd), sdt),
    ]
    if not grouped:
        scratch += [pltpu.VMEM((HDR_ROWS, QKV_DIM), F32)]
    return pl.pallas_call(
        functools.partial(_dn_kernel, tm=tm, sub=sub, chunk=chunk, grouped=grouped,
                          tiles_per_seq=tiles_per_seq, sdt=sdt, aliased=s_prev is not None),
        grid=(rows // tm,),
        input_output_aliases=aliases,
        in_specs=in_specs,
        out_specs=[row_spec(D_MODEL), tail_spec, s_out_spec],
        out_shape=[jax.ShapeDtypeStruct((rows, D_MODEL), F32),
                   jax.ShapeDtypeStruct((tail_rows, QKV_DIM), F32),
                   jax.ShapeDtypeStruct(s_out_shape, F32)],
        scratch_shapes=scratch,
        compiler_params=pltpu.CompilerParams(dimension_semantics=("arbitrary",),
                                             vmem_limit_bytes=VMEM_LIMIT_BYTES),
        name="dn_grouped" if grouped else "dn_seq",
    )(*args)


def _pad_state(state, width):
    b, _, c = state.shape
    return jnp.pad(state, ((0, 0), (SUBLANES - (width - 1), 0), (0, 0))).reshape(b * SUBLANES, c)


def _tail_state(tail, width):
    c = tail.shape[-1]
    return tail.reshape(-1, SUBLANES, c)[:, SUBLANES - (width - 1):, :]


def _row(v):
    return v.reshape(1, -1).astype(F32)


def kernel(x_prompt, x_sample, p_prompt, p_sample, state_dn_conv, state_dn_S, state_sc_conv, state_ffn_conv, dn_w_in, dn_conv_w, dn_a_log, dn_dt_bias, dn_norm_g, dn_w_out, sc_w_in, sc_conv_w, sc_w_out, ffn_w_up, ffn_conv_w, ffn_w_down, ln_mix_g, ln_mix_b, ln_ffn_g, ln_ffn_b, ple_w_proj, ple_w_gate):
    bp, lp, _ = x_prompt.shape
    bs, ls, _ = x_sample.shape
    assert ls == SUBLANES and lp % 512 == 0
    hd = N_HEADS * D_HEAD
    xp = x_prompt.reshape(bp * lp, D_MODEL)
    xs = x_sample.reshape(bs * ls, D_MODEL)
    consts_p = _dn_constants(128, CHUNK)
    consts_s = _dn_constants(64, min(CHUNK, ls))
    pp = p_prompt.reshape(DEPTH, bp * lp, D_PLE)
    ps = p_sample.reshape(DEPTH, bs * ls, D_PLE)

    outs_p = {"dn_conv": [], "dn_s": [], "sc": [], "ffn": []}
    outs_s = {"dn_conv": [], "sc": [], "ffn": []}
    s_new_sample = None
    for i in range(DEPTH):
        j = i // 2
        if i % 2 == 0:
            w_in = dn_w_in[j]
            w_ba = w_in[:, QKV_DIM + hd:]
            lane_pad = LANES - 2 * N_HEADS
            zeros8 = jnp.zeros((N_HEADS,), F32)
            alog16 = jnp.concatenate([zeros8, dn_a_log[j].astype(F32)])
            dtb16 = jnp.concatenate([zeros8, dn_dt_bias[j].astype(F32)])
            weights = [
                jnp.pad(w_in, ((0, 0), (0, lane_pad))).astype(BF16),
                w_ba.T.astype(BF16),
                dn_conv_w[j].astype(F32),
                jnp.pad(alog16, (0, lane_pad)).reshape(1, LANES),
                jnp.pad(dtb16, (0, lane_pad)).reshape(1, LANES),
                jnp.broadcast_to(alog16[:, None], (2 * N_HEADS, 256)),
                jnp.broadcast_to(dtb16[:, None], (2 * N_HEADS, 256)),
                _row(jnp.tile(dn_norm_g[j], N_HEADS)),
                dn_w_out[j].astype(BF16),
                _row(ln_mix_g[i]), _row(ln_mix_b[i]),
            ]
            xp, tail, s_out = _dn_call(xp, None, None, weights, consts_p, tm=512, sub=128, chunk=CHUNK, seq_len=lp)
            outs_p["dn_conv"].append(_tail_state(tail, DN_CONV))
            outs_p["dn_s"].append(s_out)
            xs, tail, s_new_sample = _dn_call(xs, _pad_state(state_dn_conv[j], DN_CONV), state_dn_S, weights,
                                              consts_s, tm=64, sub=64, chunk=ls, seq_len=ls,
                                              layer=j, s_prev=s_new_sample)
            outs_s["dn_conv"].append(_tail_state(tail, DN_CONV))
        else:
            weights = [sc_w_in[j].astype(BF16), sc_conv_w[j].astype(F32), sc_w_out[j].astype(BF16),
                       _row(ln_mix_g[i]), _row(ln_mix_b[i])]
            xp, tail = _sc_call(xp, None, *weights, tm=512, seq_len=lp)
            outs_p["sc"].append(_tail_state(tail, SC_CONV))
            xs, tail = _sc_call(xs, _pad_state(state_sc_conv[j], SC_CONV), *weights, tm=512, seq_len=ls)
            outs_s["sc"].append(_tail_state(tail, SC_CONV))
        weights = [ffn_w_up[i].astype(BF16), ffn_conv_w[i].astype(F32), ffn_w_down[i].astype(BF16),
                   _row(ln_ffn_g[i]), _row(ln_ffn_b[i]), ple_w_proj[i].astype(BF16), ple_w_gate[i].astype(BF16)]
        xp, tail = _ffn_call(xp, None, pp, *weights, tm=512, seq_len=lp, layer=i)
        outs_p["ffn"].append(_tail_state(tail, FFN_CONV))
        xs, tail = _ffn_call(xs, _pad_state(state_ffn_conv[i], FFN_CONV), ps, *weights, tm=256, seq_len=ls, layer=i)
        outs_s["ffn"].append(_tail_state(tail, FFN_CONV))

    return (xp.reshape(bp, lp, D_MODEL), xs.reshape(bs, ls, D_MODEL),
            jnp.stack(outs_p["dn_conv"]), jnp.stack(outs_p["dn_s"]), jnp.stack(outs_p["sc"]), jnp.stack(outs_p["ffn"]),
            jnp.stack(outs_s["dn_conv"]), s_new_sample, jnp.stack(outs_s["sc"]), jnp.stack(outs_s["ffn"]))
```

```python
import functools
import math

import jax
import jax.numpy as jnp
from jax import lax
from jax.experimental import pallas as pl
from jax.experimental.pallas import tpu as pltpu

F32 = jnp.float32
BF16 = jnp.bfloat16

D_MODEL = 1024
DEPTH = 4
N_HEADS = 8
D_HEAD = 128
QKV_DIM = 3 * N_HEADS * D_HEAD
DN_CONV = 4
CHUNK = 64
SC_CONV = 3
D_FF = 2816
FFN_CONV = 3
D_PLE = 256
ALPHA = (2.0 * DEPTH) ** 0.25
LN_EPS = 1e-5
RMS_EPS = 1e-6
L2_EPS = 1e-6

LANES = 128
SUBLANES = 8
VMEM_LIMIT_BYTES = 56 * 1024 * 1024
HDR_ROWS = 16
FFN_COLS = 256
MIX_COLS = 512


def _bdot(a, b):
    return jnp.dot(a.astype(BF16), b.astype(BF16), preferred_element_type=F32)


def _bdot_nt(a, b):
    return lax.dot_general(a.astype(BF16), b.astype(BF16), (((1,), (1,)), ((), ())),
                           preferred_element_type=F32)


def _bdot_tn(a, b):
    return lax.dot_general(a.astype(BF16), b.astype(BF16), (((0,), (0,)), ((), ())),
                           preferred_element_type=F32)


def _split3(x):
    hi = x.astype(BF16)
    r = x - hi.astype(F32)
    mid = r.astype(BF16)
    lo = (r - mid.astype(F32)).astype(BF16)
    return hi, mid, lo


def _dot3(x, e3):
    return jnp.dot(jnp.concatenate(_split3(x), axis=1), e3, preferred_element_type=F32)


def _dot2(x, e2):
    hi = x.astype(BF16)
    lo = (x - hi.astype(F32)).astype(BF16)
    return jnp.dot(jnp.concatenate([hi, lo], axis=1), e2, preferred_element_type=F32)


def _dot3_l(e, x):
    n = x.shape[1]
    r = jnp.dot(e, jnp.concatenate(_split3(x), axis=1), preferred_element_type=F32)
    return r[:, 0:n] + r[:, n:2 * n] + r[:, 2 * n:3 * n]


def _layer_norm(x, g, b):
    mu = jnp.mean(x, axis=-1, keepdims=True)
    xc = x - mu
    var = jnp.mean(xc * xc, axis=-1, keepdims=True)
    return xc * lax.rsqrt(var + LN_EPS) * g + b


def _silu(x):
    return x * jax.nn.sigmoid(x)


def _softplus(x):
    return jnp.maximum(x, 0.0) + jnp.log1p(jnp.exp(-jnp.abs(x)))


def _conv_carry(u, cols, width, cw_ref, carry_ref, tail_ref):
    tm = u.shape[0]
    head = u[0:HDR_ROWS, :]
    prev = carry_ref[:, cols]
    t = lax.broadcasted_iota(jnp.int32, head.shape, 0)
    y = None
    yh = None
    for j in range(width):
        d = width - 1 - j
        w = cw_ref[j:j + 1, cols]
        term = (u if d == 0 else pltpu.roll(u, d, 0)) * w
        y = term if y is None else y + term
        src = head if d == 0 else jnp.where(t >= d, pltpu.roll(head, d, 0), pltpu.roll(prev, d, 0))
        term = src * w
        yh = term if yh is None else yh + term
    carry_ref[:, cols] = u[tm - HDR_ROWS:tm, :]
    tail_ref[:, cols] = u[tm - SUBLANES:tm, :]
    return jnp.concatenate([yh, y[HDR_ROWS:tm, :]], axis=0)


def _conv_grouped(u, hdr, cols, width, cw_ref, tail_ref, period):
    n = u.shape[0]
    assert period & (period - 1) == 0
    t = lax.broadcasted_iota(jnp.int32, u.shape, 0) & (period - 1)
    y = None
    for j in range(width):
        d = width - 1 - j
        if d == 0:
            src = u
        else:
            src = jnp.where(t >= d, pltpu.roll(u, d, 0), pltpu.roll(hdr, (n - period + d) % n, 0))
        term = src * cw_ref[j:j + 1, cols]
        y = term if y is None else y + term
    tail_ref[:, cols] = u
    return y


def _residual_ln_ple(x, f, p, g_ref, b_ref, wproj_ref, wgate_ref):
    xn = _layer_norm(ALPHA * x + f, g_ref[...], b_ref[...])
    gate = jax.nn.sigmoid(_bdot(xn, wgate_ref[...]))
    return xn + _bdot(p, wproj_ref[...]) * gate


def _ffn_kernel(*refs, tm, grouped, tiles_per_seq):
    if grouped:
        (x_ref, hdr_ref, p_ref, wup_ref, cw_ref, wdn_ref, g_ref, b_ref, wproj_ref, wgate_ref,
         y_ref, tail_ref, h_s) = refs
    else:
        (x_ref, p_ref, wup_ref, cw_ref, wdn_ref, g_ref, b_ref, wproj_ref, wgate_ref,
         y_ref, tail_ref, h_s, carry_s) = refs

        @pl.when(pl.program_id(0) % tiles_per_seq == 0)
        def _():
            carry_s[...] = jnp.zeros_like(carry_s)

    x = x_ref[...]
    xb = x.astype(BF16)

    def conv(u, c0):
        cols = slice(c0, c0 + FFN_COLS)
        if grouped:
            return _conv_grouped(u, hdr_ref[:, cols], cols, FFN_CONV, cw_ref, tail_ref, SUBLANES)
        return _conv_carry(u, cols, FFN_CONV, cw_ref, carry_s, tail_ref)

    for c in range(D_FF // FFN_COLS):
        c0 = c * FFN_COLS
        val = conv(jnp.dot(xb, wup_ref[:, c0:c0 + FFN_COLS], preferred_element_type=F32), c0)
        gat = conv(jnp.dot(xb, wup_ref[:, D_FF + c0:D_FF + c0 + FFN_COLS], preferred_element_type=F32),
                   D_FF + c0)
        h_s[:, c0:c0 + FFN_COLS] = (_silu(gat) * val).astype(BF16)

    f = jnp.dot(h_s[...], wdn_ref[...], preferred_element_type=F32)
    y_ref[...] = _residual_ln_ple(x, f, p_ref[...], g_ref, b_ref, wproj_ref, wgate_ref)


def _const_spec(shape):
    zeros = (0,) * len(shape)
    return pl.BlockSpec(shape, lambda i: zeros, pipeline_mode=pl.Buffered(1))


def _ffn_call(x, hdr, p, wup, cw, wdn, g, b, wproj, wgate, *, tm, seq_len, layer):
    rows = x.shape[0]
    grouped = hdr is not None
    row_spec = lambda w: pl.BlockSpec((tm, w), lambda i: (i, 0))
    weights = [wup, cw, wdn, g, b, wproj, wgate]
    p_spec = pl.BlockSpec((None, tm, D_PLE), lambda i: (layer, i, 0))
    in_specs = [row_spec(D_MODEL)] + ([row_spec(2 * D_FF)] if grouped else []) + [p_spec]
    in_specs += [_const_spec(w.shape) for w in weights]
    args = [x] + ([hdr] if grouped else []) + [p] + weights
    scratch = [pltpu.VMEM((tm, D_FF), BF16)]
    if grouped:
        tail_rows = rows
        tail_spec = row_spec(2 * D_FF)
        tiles_per_seq = 1
    else:
        tiles_per_seq = seq_len // tm
        tail_rows = rows // seq_len * SUBLANES
        tail_spec = pl.BlockSpec((SUBLANES, 2 * D_FF), lambda i: (i // tiles_per_seq, 0))
        scratch += [pltpu.VMEM((HDR_ROWS, 2 * D_FF), F32)]
    return pl.pallas_call(
        functools.partial(_ffn_kernel, tm=tm, grouped=grouped, tiles_per_seq=tiles_per_seq),
        grid=(rows // tm,),
        in_specs=in_specs,
        out_specs=[row_spec(D_MODEL), tail_spec],
        out_shape=[jax.ShapeDtypeStruct((rows, D_MODEL), F32),
                   jax.ShapeDtypeStruct((tail_rows, 2 * D_FF), F32)],
        scratch_shapes=scratch,
        compiler_params=pltpu.CompilerParams(dimension_semantics=("arbitrary",),
                                             vmem_limit_bytes=VMEM_LIMIT_BYTES),
        name="ffn_grouped" if grouped else "ffn_seq",
    )(*args)


def _sc_kernel(*refs, tm, grouped, tiles_per_seq):
    if grouped:
        x_ref, hdr_ref, win_ref, cw_ref, wout_ref, g_ref, b_ref, y_ref, tail_ref, h_s = refs
    else:
        x_ref, win_ref, cw_ref, wout_ref, g_ref, b_ref, y_ref, tail_ref, h_s, carry_s = refs

        @pl.when(pl.program_id(0) % tiles_per_seq == 0)
        def _():
            carry_s[...] = jnp.zeros_like(carry_s)

    x = x_ref[...]
    xb = x.astype(BF16)
    for c in range(D_MODEL // MIX_COLS):
        c0 = c * MIX_COLS
        cols = slice(c0, c0 + MIX_COLS)
        bb = jnp.dot(xb, win_ref[:, c0:c0 + MIX_COLS], preferred_element_type=F32)
        cc = jnp.dot(xb, win_ref[:, D_MODEL + c0:D_MODEL + c0 + MIX_COLS], preferred_element_type=F32)
        hh = jnp.dot(xb, win_ref[:, 2 * D_MODEL + c0:2 * D_MODEL + c0 + MIX_COLS], preferred_element_type=F32)
        u = cc * hh
        if grouped:
            uc = _conv_grouped(u, hdr_ref[:, cols], cols, SC_CONV, cw_ref, tail_ref, SUBLANES)
        else:
            uc = _conv_carry(u, cols, SC_CONV, cw_ref, carry_s, tail_ref)
        h_s[:, cols] = (bb * uc).astype(BF16)
    mix = jnp.dot(h_s[...], wout_ref[...], preferred_element_type=F32)
    y_ref[...] = _layer_norm(ALPHA * x + mix, g_ref[...], b_ref[...])


def _sc_call(x, hdr, win, cw, wout, g, b, *, tm, seq_len):
    rows = x.shape[0]
    grouped = hdr is not None
    row_spec = lambda w: pl.BlockSpec((tm, w), lambda i: (i, 0))
    weights = [win, cw, wout, g, b]
    in_specs = [row_spec(D_MODEL)] + ([row_spec(D_MODEL)] if grouped else [])
    in_specs += [_const_spec(w.shape) for w in weights]
    args = [x] + ([hdr] if grouped else []) + weights
    scratch = [pltpu.VMEM((tm, D_MODEL), BF16)]
    if grouped:
        tail_rows = rows
        tail_spec = row_spec(D_MODEL)
        tiles_per_seq = 1
    else:
        tiles_per_seq = seq_len // tm
        tail_rows = rows // seq_len * SUBLANES
        tail_spec = pl.BlockSpec((SUBLANES, D_MODEL), lambda i: (i // tiles_per_seq, 0))
        scratch += [pltpu.VMEM((HDR_ROWS, D_MODEL), F32)]
    return pl.pallas_call(
        functools.partial(_sc_kernel, tm=tm, grouped=grouped, tiles_per_seq=tiles_per_seq),
        grid=(rows // tm,),
        in_specs=in_specs,
        out_specs=[row_spec(D_MODEL), tail_spec],
        out_shape=[jax.ShapeDtypeStruct((rows, D_MODEL), F32),
                   jax.ShapeDtypeStruct((tail_rows, D_MODEL), F32)],
        scratch_shapes=scratch,
        compiler_params=pltpu.CompilerParams(dimension_semantics=("arbitrary",),
                                             vmem_limit_bytes=VMEM_LIMIT_BYTES),
        name="sc_grouped" if grouped else "sc_seq",
    )(*args)


def _dn_constants(sub, chunk):
    r = jnp.arange(sub)
    same = (r[:, None] // chunk) == (r[None, :] // chunk)
    causal = same & (r[:, None] >= r[None, :])
    strict = same & (r[:, None] > r[None, :])
    lane = jnp.arange(LANES)
    col = jnp.arange(N_HEADS * D_HEAD)
    head_of_col = col // D_HEAD
    seg = (head_of_col[:, None] == lane[None, :]).astype(BF16)
    exp_b = (lane[:, None] == head_of_col[None, :]).astype(BF16)
    exp_g = (lane[:, None] == head_of_col[None, :] + N_HEADS).astype(BF16)
    utri = causal.T.astype(BF16)
    return dict(
        ltri=causal.astype(BF16),
        utri=jnp.concatenate([utri] * 3, axis=0),
        negmask=jnp.where(causal, 0.0, -jnp.inf).astype(F32),
        strict=strict.astype(F32),
        eye=jnp.eye(sub, dtype=F32),
        seg=jnp.concatenate([seg] * 2, axis=0),
        exp_b=jnp.concatenate([exp_b] * 2, axis=0),
        exp_g=jnp.concatenate([exp_g] * 3, axis=0),
    )


def _dn_kernel(*refs, tm, sub, chunk, grouped, tiles_per_seq, sdt, aliased):
    n_in = 22 if grouped else 20
    ins, rest = refs[:n_in], refs[n_in:]
    if aliased:
        rest = rest[1:]
    if grouped:
        x_ref, hdr_ref, sin_ref = ins[:3]
        ins = ins[3:]
    else:
        x_ref = ins[0]
        ins = ins[1:]
    (win_ref, wbat_ref, cw_ref, alog_c_ref, dtb_c_ref, alog_r_ref, dtb_r_ref,
     ng_ref, wout_ref, g_ref, b_ref,
     ltri_ref, utri_ref, negmask_ref, strict_ref, eye_ref, seg_ref, expb_ref, expg_ref) = ins
    y_ref, tail_ref, sout_ref = rest[:3]
    scr = rest[3:]
    (qkv_s, z_s, ba_s, bat_s, o_s, k_s, kb_s, qs_s, qg_s, vb_s, kbg_s, gcb_s, gcr_s,
     att_s, u_s, w_s, vn_s) = scr[:17]
    if not grouped:
        (carry_s,) = scr[17:]

        @pl.when(pl.program_id(0) % tiles_per_seq == 0)
        def _():
            carry_s[...] = jnp.zeros_like(carry_s)
            sout_ref[...] = jnp.zeros_like(sout_ref)

    n_chunks = sub // chunk
    n_levels = int(math.log2(chunk)) - 1

    x = x_ref[...]
    xb = x.astype(BF16)
    hd = N_HEADS * D_HEAD
    proj = jnp.dot(xb, win_ref[...], preferred_element_type=F32)
    for c in range(QKV_DIM // MIX_COLS):
        c0 = c * MIX_COLS
        cols = slice(c0, c0 + MIX_COLS)
        u = proj[:, cols]
        if grouped:
            uc = _conv_grouped(u, hdr_ref[:, cols], cols, DN_CONV, cw_ref, tail_ref, SUBLANES)
        else:
            uc = _conv_carry(u, cols, DN_CONV, cw_ref, carry_s, tail_ref)
        qkv_s[:, cols] = _silu(uc)
    z_s[...] = proj[:, QKV_DIM:QKV_DIM + hd]
    ba_s[...] = proj[:, QKV_DIM + hd:QKV_DIM + hd + LANES]
    bat_s[...] = lax.dot_general(wbat_ref[...], xb, (((1,), (1,)), ((), ())),
                                 preferred_element_type=F32)

    seg = seg_ref[...]
    expb = expb_ref[...]
    expg = expg_ref[...]
    hd = N_HEADS * D_HEAD

    def sub_tile(si, carry):
        rb = pl.multiple_of(si * sub, sub)
        rows = pl.ds(rb, sub)
        q = qkv_s[rows, 0:hd]
        k = qkv_s[rows, hd:2 * hd]
        v = qkv_s[rows, 2 * hd:3 * hd]
        rq = lax.rsqrt(_dot2(q * q, seg) + L2_EPS)
        rk = lax.rsqrt(_dot2(k * k, seg) + L2_EPS)
        qn = q * _dot2(rq, expb)
        kn = k * _dot2(rk, expb)

        ba = ba_s[rows, :]
        beta_c = jax.nn.sigmoid(ba)
        g_c = -jnp.exp(alog_c_ref[...]) * _softplus(ba + dtb_c_ref[...])
        gc_c = _dot3_l(ltri_ref[...], g_c)
        beta_b = _dot2(beta_c, expb)
        gc_b = _dot3(gc_c, expg)
        eg_b = jnp.exp(gc_b)

        bat = bat_s[:, rows]
        g_r = -jnp.exp(alog_r_ref[:, 0:sub]) * _softplus(bat + dtb_r_ref[:, 0:sub])
        gcr_s[...] = _dot3(g_r, utri_ref[...])

        kbeta = kn * beta_b
        k_s[...] = kn.astype(sdt)
        kb_s[...] = kbeta.astype(sdt)
        qs = qn * (D_HEAD ** -0.5)
        qs_s[...] = qs.astype(sdt)
        qg_s[...] = (qs * eg_b).astype(sdt)
        vb_s[...] = (v * beta_b).astype(sdt)
        kbg_s[...] = (kbeta * eg_b).astype(sdt)
        gcb_s[...] = gc_b

        negmask = negmask_ref[...]
        strict = strict_ref[...]
        eye = eye_ref[...]
        heads = range(N_HEADS)
        hsl = [slice(h * D_HEAD, (h + 1) * D_HEAD) for h in heads]
        kk = [_bdot_nt(kb_s[:, hsl[h]], k_s[:, hsl[h]]) for h in heads]
        qk = [_bdot_nt(qs_s[:, hsl[h]], k_s[:, hsl[h]]) for h in heads]
        bm, pm = [], []
        for h in heads:
            gcb = gcb_s[:, hsl[h]]
            gcw = gcb[:, 0:sub] if sub <= D_HEAD else jnp.concatenate([gcb] * (sub // D_HEAD), axis=1)
            gcr = gcr_s[N_HEADS + h:N_HEADS + h + 1, :]
            dec = jnp.exp(gcw - gcr + negmask)
            att_s[h] = (qk[h] * dec).astype(sdt)
            b0 = -(kk[h] * dec * strict)
            bm.append(b0)
            pm.append(eye + b0)
        bm = [_bdot(b, b) for b in bm]
        for _ in range(1, n_levels):
            r = [_bdot(bm[h], jnp.concatenate([bm[h], pm[h]], axis=1)) for h in heads]
            bm = [r[h][:, 0:sub] for h in heads]
            pm = [pm[h] + r[h][:, sub:2 * sub] for h in heads]
        pm = [pm[h] + _bdot(bm[h], pm[h]) for h in heads]
        sol = [_bdot(pm[h], jnp.concatenate([vb_s[:, hsl[h]], kbg_s[:, hsl[h]]], axis=1)) for h in heads]
        for h in heads:
            u_s[:, hsl[h]] = sol[h][:, 0:D_HEAD]
            w_s[:, hsl[h]] = sol[h][:, D_HEAD:2 * D_HEAD]
        vn_s[...] = jnp.zeros_like(vn_s)

        def chunk_step(c, carry_c):
            r0 = pl.multiple_of(c * chunk, chunk)
            cr = pl.ds(r0, chunk)
            if grouped:
                s_old = [sin_ref[si * n_chunks + c, h] for h in heads]
            else:
                s_old = [sout_ref[h] for h in heads]
            ws = [_bdot(jnp.concatenate([w_s[cr, hsl[h]], qg_s[cr, hsl[h]].astype(F32)], axis=0), s_old[h])
                  for h in heads]
            v_new = [u_s[cr, hsl[h]] - ws[h][0:chunk] for h in heads]
            for h in heads:
                vn_s[cr, hsl[h]] = v_new[h].astype(sdt)
            for h in heads:
                o_s[pl.ds(rb + r0, chunk), hsl[h]] = (ws[h][chunk:2 * chunk]
                                                      + _bdot(att_s[h, cr, :], vn_s[:, hsl[h]]))
            for h in heads:
                gc_rows = gcb_s[cr, hsl[h]]
                g_last = gc_rows[chunk - 1:chunk, :]
                kd = k_s[cr, hsl[h]].astype(F32) * jnp.exp(g_last - gc_rows)
                s_new = s_old[h] * jnp.exp(g_last) + _bdot_tn(kd, v_new[h])
                if grouped:
                    sout_ref[si * n_chunks + c, h] = s_new
                else:
                    sout_ref[h] = s_new
            return carry_c

        lax.fori_loop(0, n_chunks, chunk_step, 0, unroll=n_chunks <= 2)
        return carry

    lax.fori_loop(0, tm // sub, sub_tile, 0)

    o = o_s[...]
    ms = _dot2(o * o, seg) * (1.0 / D_HEAD)
    o = o * _dot2(lax.rsqrt(ms + RMS_EPS), expb) * ng_ref[...] * _silu(z_s[...])
    mix = _bdot(o, wout_ref[...])
    y_ref[...] = _layer_norm(ALPHA * x + mix, g_ref[...], b_ref[...])


def _dn_call(x, hdr, s_in, weights, consts, *, tm, sub, chunk, seq_len, layer=0, s_prev=None):
    rows = x.shape[0]
    grouped = hdr is not None
    aliases = {}
    sdt = BF16 if chunk % (2 * SUBLANES) == 0 else F32
    row_spec = lambda w: pl.BlockSpec((tm, w), lambda i: (i, 0))
    const_list = [consts[k] for k in ("ltri", "utri", "negmask", "strict", "eye", "seg", "exp_b", "exp_g")]
    in_specs = [row_spec(D_MODEL)]
    args = [x]
    hd = N_HEADS * D_HEAD
    if grouped:
        seqs_per_tile = tm // chunk
        s_spec = pl.BlockSpec((None, seqs_per_tile, N_HEADS, D_HEAD, D_HEAD), lambda i: (layer, i, 0, 0, 0))
        in_specs += [row_spec(QKV_DIM), s_spec]
        args += [hdr, s_in]
        tail_rows, tail_spec = rows, row_spec(QKV_DIM)
        s_out_shape = s_in.shape
        s_out_spec = s_spec
        tiles_per_seq = 1
    else:
        tiles_per_seq = seq_len // tm
        n_seq = rows // seq_len
        tail_rows = n_seq * SUBLANES
        tail_spec = pl.BlockSpec((SUBLANES, QKV_DIM), lambda i: (i // tiles_per_seq, 0))
        s_out_shape = (n_seq, N_HEADS, D_HEAD, D_HEAD)
        s_out_spec = pl.BlockSpec((None, N_HEADS, D_HEAD, D_HEAD), lambda i: (i // tiles_per_seq, 0, 0, 0))
    in_specs += [_const_spec(w.shape) for w in list(weights) + const_list]
    args += list(weights) + const_list
    if s_prev is not None:
        in_specs.append(pl.BlockSpec(memory_space=pl.ANY))
        args.append(s_prev)
        aliases = {len(args) - 1: 2}
    scratch = [
        pltpu.VMEM((tm, QKV_DIM), F32),
        pltpu.VMEM((tm, hd), F32),
        pltpu.VMEM((tm, LANES), F32),
        pltpu.VMEM((2 * N_HEADS, tm), F32),
        pltpu.VMEM((tm, hd), F32),
        pltpu.VMEM((sub, hd), sdt),
        pltpu.VMEM((sub, hd), sdt),
        pltpu.VMEM((sub, hd), sdt),
        pltpu.VMEM((sub, hd), sdt),
        pltpu.VMEM((sub, hd), sdt),
        pltpu.VMEM((sub, hd), sdt),
        pltpu.VMEM((sub, hd), F32),
        pltpu.VMEM((2 * N_HEADS, sub), F32),
        pltpu.VMEM((N_HEADS, sub, sub), sdt),
        pltpu.VMEM((sub, hd), F32),
        pltpu.VMEM((sub, hd), F32),
        pltpu.VMEM((sub, hd), sdt),
    ]
    if not grouped:
        scratch += [pltpu.VMEM((HDR_ROWS, QKV_DIM), F32)]
    return pl.pallas_call(
        functools.partial(_dn_kernel, tm=tm, sub=sub, chunk=chunk, grouped=grouped,
                          tiles_per_seq=tiles_per_seq, sdt=sdt, aliased=s_prev is not None),
        grid=(rows // tm,),
        input_output_aliases=aliases,
        in_specs=in_specs,
        out_specs=[row_spec(D_MODEL), tail_spec, s_out_spec],
        out_shape=[jax.ShapeDtypeStruct((rows, D_MODEL), F32),
                   jax.ShapeDtypeStruct((tail_rows, QKV_DIM), F32),
                   jax.ShapeDtypeStruct(s_out_shape, F32)],
        scratch_shapes=scratch,
        compiler_params=pltpu.CompilerParams(dimension_semantics=("arbitrary",),
                                             vmem_limit_bytes=VMEM_LIMIT_BYTES),
        name="dn_grouped" if grouped else "dn_seq",
    )(*args)


def _pad_state(state, width):
    b, _, c = state.shape
    return jnp.pad(state, ((0, 0), (SUBLANES - (width - 1), 0), (0, 0))).reshape(b * SUBLANES, c)


def _tail_state(tail, width):
    c = tail.shape[-1]
    return tail.reshape(-1, SUBLANES, c)[:, SUBLANES - (width - 1):, :]


def _row(v):
    return v.reshape(1, -1).astype(F32)


def kernel(x_prompt, x_sample, p_prompt, p_sample, state_dn_conv, state_dn_S, state_sc_conv, state_ffn_conv, dn_w_in, dn_conv_w, dn_a_log, dn_dt_bias, dn_norm_g, dn_w_out, sc_w_in, sc_conv_w, sc_w_out, ffn_w_up, ffn_conv_w, ffn_w_down, ln_mix_g, ln_mix_b, ln_ffn_g, ln_ffn_b, ple_w_proj, ple_w_gate):
    bp, lp, _ = x_prompt.shape
    bs, ls, _ = x_sample.shape
    assert ls == SUBLANES and lp % 512 == 0
    hd = N_HEADS * D_HEAD
    xp = x_prompt.reshape(bp * lp, D_MODEL)
    xs = x_sample.reshape(bs * ls, D_MODEL)
    consts_p = _dn_constants(128, CHUNK)
    consts_s = _dn_constants(64, min(CHUNK, ls))
    pp = p_prompt.reshape(DEPTH, bp * lp, D_PLE)
    ps = p_sample.reshape(DEPTH, bs * ls, D_PLE)

    outs_p = {"dn_conv": [], "dn_s": [], "sc": [], "ffn": []}
    outs_s = {"dn_conv": [], "sc": [], "ffn": []}
    s_new_sample = None
    for i in range(DEPTH):
        j = i // 2
        if i % 2 == 0:
            w_in = dn_w_in[j]
            w_ba = w_in[:, QKV_DIM + hd:]
            lane_pad = LANES - 2 * N_HEADS
            zeros8 = jnp.zeros((N_HEADS,), F32)
            alog16 = jnp.concatenate([zeros8, dn_a_log[j].astype(F32)])
            dtb16 = jnp.concatenate([zeros8, dn_dt_bias[j].astype(F32)])
            weights = [
                jnp.pad(w_in, ((0, 0), (0, lane_pad))).astype(BF16),
                w_ba.T.astype(BF16),
                dn_conv_w[j].astype(F32),
                jnp.pad(alog16, (0, lane_pad)).reshape(1, LANES),
                jnp.pad(dtb16, (0, lane_pad)).reshape(1, LANES),
                jnp.broadcast_to(alog16[:, None], (2 * N_HEADS, 256)),
                jnp.broadcast_to(dtb16[:, None], (2 * N_HEADS, 256)),
                _row(jnp.tile(dn_norm_g[j], N_HEADS)),
                dn_w_out[j].astype(BF16),
                _row(ln_mix_g[i]), _row(ln_mix_b[i]),
            ]
            xp, tail, s_out = _dn_call(xp, None, None, weights, consts_p, tm=512, sub=128, chunk=CHUNK, seq_len=lp)
            outs_p["dn_conv"].append(_tail_state(tail, DN_CONV))
            outs_p["dn_s"].append(s_out)
            xs, tail, s_new_sample = _dn_call(xs, _pad_state(state_dn_conv[j], DN_CONV), state_dn_S, weights,
                                              consts_s, tm=64, sub=64, chunk=ls, seq_len=ls,
                                              layer=j, s_prev=s_new_sample)
            outs_s["dn_conv"].append(_tail_state(tail, DN_CONV))
        else:
            weights = [sc_w_in[j].astype(BF16), sc_conv_w[j].astype(F32), sc_w_out[j].astype(BF16),
                       _row(ln_mix_g[i]), _row(ln_mix_b[i])]
            xp, tail = _sc_call(xp, None, *weights, tm=512, seq_len=lp)
            outs_p["sc"].append(_tail_state(tail, SC_CONV))
            xs, tail = _sc_call(xs, _pad_state(state_sc_conv[j], SC_CONV), *weights, tm=512, seq_len=ls)
            outs_s["sc"].append(_tail_state(tail, SC_CONV))
        weights = [ffn_w_up[i].astype(BF16), ffn_conv_w[i].astype(F32), ffn_w_down[i].astype(BF16),
                   _row(ln_ffn_g[i]), _row(ln_ffn_b[i]), ple_w_proj[i].astype(BF16), ple_w_gate[i].astype(BF16)]
        xp, tail = _ffn_call(xp, None, pp, *weights, tm=512, seq_len=lp, layer=i)
        outs_p["ffn"].append(_tail_state(tail, FFN_CONV))
        xs, tail = _ffn_call(xs, _pad_state(state_ffn_conv[i], FFN_CONV), ps, *weights, tm=256, seq_len=ls, layer=i)
        outs_s["ffn"].append(_tail_state(tail, FFN_CONV))

    return (xp.reshape(bp, lp, D_MODEL), xs.reshape(bs, ls, D_MODEL),
            jnp.stack(outs_p["dn_conv"]), jnp.stack(outs_p["dn_s"]), jnp.stack(outs_p["sc"]), jnp.stack(outs_p["ffn"]),
            jnp.stack(outs_s["dn_conv"]), s_new_sample, jnp.stack(outs_s["sc"]), jnp.stack(outs_s["ffn"]))
```
